```python
import jax, jax.numpy as jnp
from jax import lax
import numpy as np

D_MODEL = 4096
BATCH = 4
SEQ = 2048
DEPTH = 1
DEC_BATCH = 16
DEC_SEQ = 16
PAST_LEN = 2048

CHUNK = 64
N_PREV_CHUNKS = 8
N_BAND = N_PREV_CHUNKS + 1
ATT_WINDOW = N_PREV_CHUNKS * CHUNK
D_ATT = D_MODEL // 2
D_POOL = D_MODEL - D_ATT
D_MIX = D_ATT + D_POOL
N_HEADS = 16
HEAD_DIM = D_ATT // N_HEADS
REL_CLIP = 256
POOL_WINDOWS = (2, 4, 8, 16)
N_POOL_GROUPS = len(POOL_WINDOWS)
POOL_GROUP_W = D_POOL // N_POOL_GROUPS
POOL_HIST = max(POOL_WINDOWS) - 1
D_FF = -(-8 * D_MODEL // (3 * 256)) * 256
LN_EPS = 1e-5
DEEPNORM_ALPHA = (2.0 * DEPTH) ** 0.25
DEEPNORM_BETA = (8.0 * DEPTH) ** -0.25
NEG_INF = -1e30

kernel_name = "hybrid_chunk_attn_pool_encoder_step"


def layer_norm(x, g, b):
    xf = x.astype(jnp.float32)
    mu = jnp.mean(xf, axis=-1, keepdims=True)
    var = jnp.mean(jnp.square(xf - mu), axis=-1, keepdims=True)
    y = (xf - mu) * lax.rsqrt(var + LN_EPS) * g.astype(jnp.float32) + b.astype(jnp.float32)
    return y.astype(x.dtype)


def project_in(x, w_in):
    b, t, _ = x.shape
    h = x @ w_in
    q = h[..., :D_ATT].reshape(b, t, N_HEADS, HEAD_DIM)
    k = h[..., D_ATT:2 * D_ATT].reshape(b, t, N_HEADS, HEAD_DIM)
    v = h[..., 2 * D_ATT:3 * D_ATT].reshape(b, t, N_HEADS, HEAD_DIM)
    u = h[..., 3 * D_ATT:]
    return q, k, v, u


def rel_bias_lookup(table, q_offset, tq, tk):
    dist = q_offset + jnp.arange(tq)[:, None] - jnp.arange(tk)[None, :]
    idx = jnp.clip(dist, -REL_CLIP, REL_CLIP) + REL_CLIP
    return table[:, idx]


def band_attention(q, k, v, bias, mask):
    s = jnp.einsum('bcqhd,bckhd->bchqk', q, k).astype(jnp.float32) * (HEAD_DIM ** -0.5)
    s = s + bias.astype(jnp.float32)[None, None]
    s = jnp.where(mask[None, :, None, None, :], s, NEG_INF)
    p = jax.nn.softmax(s, axis=-1).astype(v.dtype)
    return jnp.einsum('bchqk,bckhd->bcqhd', p, v)


def prompt_attention(q, k, v, table):
    b, s = q.shape[:2]
    nc = s // CHUNK
    pad = ((0, 0), (ATT_WINDOW, 0), (0, 0), (0, 0))
    kp = jnp.pad(k, pad).reshape(b, nc + N_PREV_CHUNKS, CHUNK, N_HEADS, HEAD_DIM)
    vp = jnp.pad(v, pad).reshape(b, nc + N_PREV_CHUNKS, CHUNK, N_HEADS, HEAD_DIM)
    kb = jnp.concatenate([kp[:, j:j + nc] for j in range(N_BAND)], axis=2)
    vb = jnp.concatenate([vp[:, j:j + nc] for j in range(N_BAND)], axis=2)
    qc = q.reshape(b, nc, CHUNK, N_HEADS, HEAD_DIM)
    bias = rel_bias_lookup(table, ATT_WINDOW, CHUNK, N_BAND * CHUNK)
    key_pos = (jnp.arange(nc)[:, None] - N_PREV_CHUNKS) * CHUNK + jnp.arange(N_BAND * CHUNK)[None, :]
    mask = key_pos >= 0
    o = band_attention(qc, kb, vb, bias, mask)
    return o.reshape(b, s, D_ATT)


def sample_attention(q, k, v, cache_k, cache_v, table):
    b, t = q.shape[:2]
    lc = cache_k.shape[1]
    kk = jnp.concatenate([cache_k, k], axis=1)[:, None]
    vv = jnp.concatenate([cache_v, v], axis=1)[:, None]
    bias = rel_bias_lookup(table, lc, t, lc + t)
    mask = jnp.ones((1, lc + t), dtype=bool)
    o = band_attention(q[:, None], kk, vv, bias, mask)
    return o[:, 0].reshape(b, t, D_ATT)


def pool_mix(u, hist, hist_len, w_pool, pool_scale):
    b, t, _ = u.shape
    full = jnp.concatenate([hist, u], axis=1).astype(jnp.float32)
    cs = jnp.pad(lax.cumsum(full, axis=1), ((0, 0), (1, 0), (0, 0)))
    pos = jnp.arange(t)
    outs = []
    for g, w in enumerate(POOL_WINDOWS):
        sl = slice(g * POOL_GROUP_W, (g + 1) * POOL_GROUP_W)
        csg = cs[..., sl]
        wsum = csg[:, POOL_HIST + 1:POOL_HIST + 1 + t] - csg[:, POOL_HIST + 1 - w:POOL_HIST + 1 - w + t]
        cnt = jnp.minimum(w, pos + 1 + hist_len).astype(jnp.float32)[None, :, None]
        outs.append(wsum / cnt - full[:, POOL_HIST:, sl])
    d = jnp.stack(outs, axis=2).astype(u.dtype)
    y = jnp.einsum('btgc,gcd->btgd', d, w_pool).reshape(b, t, D_POOL)
    return y * pool_scale


def finish_layer(x, att_o, pool_o, w_out, ln1_g, ln1_b, w_gate, w_up, w_down, ln2_g, ln2_b):
    h = jnp.concatenate([att_o, pool_o], axis=-1) @ w_out
    x1 = layer_norm(DEEPNORM_ALPHA * x + h, ln1_g, ln1_b)
    f = (jax.nn.silu(x1 @ w_gate) * (x1 @ w_up)) @ w_down
    return layer_norm(DEEPNORM_ALPHA * x1 + f, ln2_g, ln2_b)


def setup_inputs(seed: int = 0) -> dict:
    key = jax.random.key(seed)
    ks = jax.random.split(key, 18)
    nrm = jax.random.normal
    lc = min(ATT_WINDOW, PAST_LEN)
    return {
        'x_prompt': nrm(ks[0], (BATCH, SEQ, D_MODEL), jnp.float32),
        'x_sample': nrm(ks[1], (DEC_BATCH, DEC_SEQ, D_MODEL), jnp.float32),
        'cache_k': nrm(ks[2], (DEPTH, DEC_BATCH, lc, N_HEADS, HEAD_DIM), jnp.float32),
        'cache_v': nrm(ks[3], (DEPTH, DEC_BATCH, lc, N_HEADS, HEAD_DIM), jnp.float32),
        'state_pool': nrm(ks[4], (DEPTH, DEC_BATCH, POOL_HIST, D_POOL), jnp.float32),
        'w_in': nrm(ks[5], (DEPTH, D_MODEL, 3 * D_ATT + D_POOL), jnp.float32) * D_MODEL ** -0.5,
        'rel_bias': 0.5 * nrm(ks[6], (DEPTH, N_HEADS, 2 * REL_CLIP + 1), jnp.float32),
        'w_pool': nrm(ks[7], (DEPTH, N_POOL_GROUPS, POOL_GROUP_W, POOL_GROUP_W), jnp.float32) * POOL_GROUP_W ** -0.5,
        'pool_scale': 1.0 + 0.1 * nrm(ks[8], (DEPTH, D_POOL), jnp.float32),
        'w_out': nrm(ks[9], (DEPTH, D_MIX, D_MODEL), jnp.float32) * (D_MIX ** -0.5 * DEEPNORM_BETA),
        'ln1_g': 1.0 + 0.1 * nrm(ks[10], (DEPTH, D_MODEL), jnp.float32),
        'ln1_b': 0.01 * nrm(ks[11], (DEPTH, D_MODEL), jnp.float32),
        'w_gate': nrm(ks[12], (DEPTH, D_MODEL, D_FF), jnp.float32) * D_MODEL ** -0.5,
        'w_up': nrm(ks[13], (DEPTH, D_MODEL, D_FF), jnp.float32) * D_MODEL ** -0.5,
        'w_down': nrm(ks[14], (DEPTH, D_FF, D_MODEL), jnp.float32) * (D_FF ** -0.5 * DEEPNORM_BETA),
        'ln2_g': 1.0 + 0.1 * nrm(ks[15], (DEPTH, D_MODEL), jnp.float32),
        'ln2_b': 0.01 * nrm(ks[16], (DEPTH, D_MODEL), jnp.float32),
    }


def reference(x_prompt, x_sample, cache_k, cache_v, state_pool, w_in, rel_bias, w_pool, pool_scale,
              w_out, ln1_g, ln1_b, w_gate, w_up, w_down, ln2_g, ln2_b):
    yp, ys = x_prompt, x_sample
    kp_new, vp_new, pp_new, ks_new, vs_new, ps_new = [], [], [], [], [], []
    keep_p = min(ATT_WINDOW, x_prompt.shape[1])
    for l in range(DEPTH):
        qp, kp, vp, up = project_in(yp, w_in[l])
        att_p = prompt_attention(qp, kp, vp, rel_bias[l])
        hist0 = jnp.zeros((up.shape[0], POOL_HIST, D_POOL), up.dtype)
        pool_p = pool_mix(up, hist0, 0, w_pool[l], pool_scale[l])
        qs, k_s, v_s, us = project_in(ys, w_in[l])
        att_s = sample_attention(qs, k_s, v_s, cache_k[l], cache_v[l], rel_bias[l])
        pool_s = pool_mix(us, state_pool[l], POOL_HIST, w_pool[l], pool_scale[l])

        kp_new.append(kp[:, -keep_p:])
        vp_new.append(vp[:, -keep_p:])
        pp_new.append(up[:, -POOL_HIST:])
        ks_new.append(k_s)
        vs_new.append(v_s)
        ps_new.append(jnp.concatenate([state_pool[l], us], axis=1)[:, -POOL_HIST:])

        yp = finish_layer(yp, att_p, pool_p, w_out[l], ln1_g[l], ln1_b[l], w_gate[l], w_up[l], w_down[l], ln2_g[l], ln2_b[l])
        ys = finish_layer(ys, att_s, pool_s, w_out[l], ln1_g[l], ln1_b[l], w_gate[l], w_up[l], w_down[l], ln2_g[l], ln2_b[l])
    return (yp, ys, jnp.stack(kp_new), jnp.stack(vp_new), jnp.stack(pp_new),
            jnp.stack(ks_new), jnp.stack(vs_new), jnp.stack(ps_new))
```

```python
import functools

import jax
import jax.numpy as jnp
from jax import lax
from jax.experimental import pallas as pl
from jax.experimental.pallas import tpu as pltpu

D_MODEL = 4096
CHUNK = 64
N_PREV_CHUNKS = 8
ATT_WINDOW = N_PREV_CHUNKS * CHUNK
D_ATT = D_MODEL // 2
D_POOL = D_MODEL - D_ATT
N_HEADS = 16
HEAD_DIM = D_ATT // N_HEADS
REL_CLIP = 256
POOL_WINDOWS = (2, 4, 8, 16)
POOL_GROUP_W = D_POOL // len(POOL_WINDOWS)
POOL_HIST = max(POOL_WINDOWS) - 1
LN_EPS = 1e-5
DEEPNORM_ALPHA = 2.0 ** 0.25
NEG_INF = -1e30
ATT_SCALE = HEAD_DIM ** -0.5

GROUP_CHUNKS = 4
GROUP_ROWS = GROUP_CHUNKS * CHUNK
BAND_COLS = GROUP_ROWS + ATT_WINDOW

VMEM_LIMIT_BYTES = 56 * 1024 * 1024

F32 = jnp.float32
BF16 = jnp.bfloat16


def _params(semantics):
    return pltpu.CompilerParams(dimension_semantics=semantics, vmem_limit_bytes=VMEM_LIMIT_BYTES)


def _cast_kernel(w_ref, o_ref):
    o_ref[...] = w_ref[...].astype(BF16)


def _cast_bf16(w, rows):
    r, c = w.shape
    return pl.pallas_call(
        _cast_kernel,
        grid=(r // rows,),
        in_specs=[pl.BlockSpec((rows, c), lambda i: (i, 0))],
        out_specs=pl.BlockSpec((rows, c), lambda i: (i, 0)),
        out_shape=jax.ShapeDtypeStruct((r, c), BF16),
        compiler_params=_params(("arbitrary",)),
        name="cast_bf16",
    )(w)


def _in_proj_kernel(x_ref, w_ref, q_ref, k_ref, v_ref, u_ref, xb_ref, *, seg_blocks):
    j = pl.program_id(1)

    @pl.when(j == 0)
    def _():
        xb_ref[...] = x_ref[...].astype(BF16)

    acc = jnp.dot(xb_ref[...], w_ref[...], preferred_element_type=F32)
    seg = j // seg_blocks

    @pl.when(seg == 0)
    def _():
        q_ref[...] = (acc * ATT_SCALE).astype(BF16)

    @pl.when(seg == 1)
    def _():
        k_ref[...] = acc

    @pl.when(seg == 2)
    def _():
        v_ref[...] = acc

    @pl.when(seg == 3)
    def _():
        u_ref[...] = acc


def _in_proj(x, w_in_b, tm, tn):
    m = x.shape[0]
    seg_blocks = D_ATT // tn
    last = seg_blocks - 1

    def seg_map(s):
        return lambda i, j: (i, jnp.clip(j - s * seg_blocks, 0, last))

    return pl.pallas_call(
        functools.partial(_in_proj_kernel, seg_blocks=seg_blocks),
        grid=(m // tm, 4 * seg_blocks),
        in_specs=[
            pl.BlockSpec((tm, D_MODEL), lambda i, j: (i, 0), pipeline_mode=pl.Buffered(1)),
            pl.BlockSpec((D_MODEL, tn), lambda i, j: (0, j)),
        ],
        out_specs=[pl.BlockSpec((tm, tn), seg_map(s)) for s in range(4)],
        out_shape=[
            jax.ShapeDtypeStruct((m, D_ATT), BF16),
            jax.ShapeDtypeStruct((m, D_ATT), F32),
            jax.ShapeDtypeStruct((m, D_ATT), F32),
            jax.ShapeDtypeStruct((m, D_POOL), F32),
        ],
        scratch_shapes=[pltpu.VMEM((tm, D_MODEL), BF16)],
        compiler_params=_params(("arbitrary", "arbitrary")),
        name="in_proj",
    )(x, w_in_b)


def _band_bias(table):
    i = jnp.arange(GROUP_ROWS)[:, None]
    j = jnp.arange(BAND_COLS)[None, :]
    idx = jnp.clip(ATT_WINDOW + i - j, -REL_CLIP, REL_CLIP) + REL_CLIP
    chunk_gap = j // CHUNK - i // CHUNK
    in_band = (chunk_gap >= 0) & (chunk_gap <= N_PREV_CHUNKS)
    return jnp.where(in_band[None], table[:, idx], NEG_INF)


def _prompt_attn_kernel(q_ref, k_ref, v_ref, bias_ref, o_ref, kb_ref, vb_ref, *, seq):
    kb_ref[...] = k_ref[...].astype(BF16)
    vb_ref[...] = v_ref[...].astype(BF16)
    for g in range(seq // GROUP_ROWS):
        q0 = g * GROUP_ROWS
        k0 = max(q0 - ATT_WINDOW, 0)
        nk = q0 + GROUP_ROWS - k0
        c0 = BAND_COLS - nk
        q = q_ref[pl.ds(q0, GROUP_ROWS), :]
        s = lax.dot_general(q, kb_ref[pl.ds(k0, nk), :], (((1,), (1,)), ((), ())),
                            preferred_element_type=F32)
        s = s + bias_ref[0, :, c0:]
        m = jnp.max(s, axis=-1, keepdims=True)
        p = jnp.exp(s - m)
        l = jnp.sum(p, axis=-1, keepdims=True)
        o = jnp.dot(p.astype(BF16), vb_ref[pl.ds(k0, nk), :], preferred_element_type=F32)
        o_ref[pl.ds(q0, GROUP_ROWS), :] = (o / l).astype(BF16)


def _prompt_attention(q, k, v, bias, batch, seq):
    blk = lambda: pl.BlockSpec((seq, HEAD_DIM), lambda b, h: (b, h))
    return pl.pallas_call(
        functools.partial(_prompt_attn_kernel, seq=seq),
        grid=(batch, N_HEADS),
        in_specs=[blk(), blk(), blk(),
                  pl.BlockSpec((1, GROUP_ROWS, BAND_COLS), lambda b, h: (h, 0, 0))],
        out_specs=blk(),
        out_shape=jax.ShapeDtypeStruct((batch * seq, D_ATT), BF16),
        scratch_shapes=[pltpu.VMEM((seq, HEAD_DIM), BF16), pltpu.VMEM((seq, HEAD_DIM), BF16)],
        compiler_params=_params(("arbitrary", "arbitrary")),
        name="prompt_attention",
    )(q, k, v, bias)


def _sample_attn_kernel(q_ref, kn_ref, vn_ref, ck_ref, cv_ref, bias_ref, o_ref, *, lc):
    contract_last = (((1,), (1,)), ((), ()))
    for h in range(N_HEADS):
        cols = slice(h * HEAD_DIM, (h + 1) * HEAD_DIM)
        q = q_ref[:, cols]
        kc = ck_ref[0, :, cols].astype(BF16)
        vc = cv_ref[0, :, cols].astype(BF16)
        kn = kn_ref[:, cols].astype(BF16)
        vn = vn_ref[:, cols].astype(BF16)
        s_c = lax.dot_general(q, kc, contract_last, preferred_element_type=F32) + bias_ref[h, :, :lc]
        s_n = lax.dot_general(q, kn, contract_last, preferred_element_type=F32) + bias_ref[h, :, lc:]
        m = jnp.maximum(jnp.max(s_c, axis=-1, keepdims=True), jnp.max(s_n, axis=-1, keepdims=True))
        p_c = jnp.exp(s_c - m)
        p_n = jnp.exp(s_n - m)
        l = jnp.sum(p_c, axis=-1, keepdims=True) + jnp.sum(p_n, axis=-1, keepdims=True)
        o = (jnp.dot(p_c.astype(BF16), vc, preferred_element_type=F32)
             + jnp.dot(p_n.astype(BF16), vn, preferred_element_type=F32))
        o_ref[:, cols] = (o / l).astype(BF16)


def _sample_attention(q, k, v, cache_k, cache_v, bias, batch, t):
    lc = cache_k.shape[1]
    row = lambda: pl.BlockSpec((t, D_ATT), lambda b: (b, 0))
    cache = lambda: pl.BlockSpec((1, lc, D_ATT), lambda b: (b, 0, 0))
    return pl.pallas_call(
        functools.partial(_sample_attn_kernel, lc=lc),
        grid=(batch,),
        in_specs=[row(), row(), row(), cache(), cache(),
                  pl.BlockSpec((N_HEADS, t, lc + t), lambda b: (0, 0, 0))],
        out_specs=row(),
        out_shape=jax.ShapeDtypeStruct((batch * t, D_ATT), BF16),
        compiler_params=_params(("arbitrary",)),
        name="sample_attention",
    )(q, k, v, cache_k, cache_v, bias)


HALO = POOL_HIST + 1


def _prompt_pool_kernel(u_ref, halo_ref, wp_ref, scale_ref, o_ref, full_ref, *, tm):
    t = pl.program_id(1)

    @pl.when(t == 0)
    def _():
        full_ref[0:HALO, :] = jnp.zeros((HALO, D_POOL), F32)

    @pl.when(t > 0)
    def _():
        full_ref[0:HALO, :] = halo_ref[...]

    full_ref[HALO:, :] = u_ref[...]
    pos = t * tm + lax.broadcasted_iota(jnp.int32, (tm, 1), 0)
    for g, w in enumerate(POOL_WINDOWS):
        cols = slice(g * POOL_GROUP_W, (g + 1) * POOL_GROUP_W)
        wsum = full_ref[HALO:, cols]
        for s in range(1, w):
            wsum = wsum + full_ref[HALO - s:HALO - s + tm, cols]
        cnt = jnp.minimum(w, pos + 1).astype(F32)
        d = wsum / cnt - full_ref[HALO:, cols]
        y = jnp.dot(d.astype(BF16), wp_ref[g], preferred_element_type=F32)
        o_ref[:, cols] = (y * scale_ref[:, cols]).astype(BF16)


def _prompt_pool(u, w_pool_b, pool_scale, batch, seq, tm):
    tiles = seq // tm
    halo_per_tile = tm // HALO
    return pl.pallas_call(
        functools.partial(_prompt_pool_kernel, tm=tm),
        grid=(batch, tiles),
        in_specs=[
            pl.BlockSpec((tm, D_POOL), lambda b, t: (b * tiles + t, 0)),
            pl.BlockSpec((HALO, D_POOL),
                         lambda b, t: (jnp.maximum((b * tiles + t) * halo_per_tile - 1, 0), 0)),
            pl.BlockSpec((len(POOL_WINDOWS), POOL_GROUP_W, POOL_GROUP_W), lambda b, t: (0, 0, 0)),
            pl.BlockSpec((1, D_POOL), lambda b, t: (0, 0)),
        ],
        out_specs=pl.BlockSpec((tm, D_POOL), lambda b, t: (b * tiles + t, 0)),
        out_shape=jax.ShapeDtypeStruct((batch * seq, D_POOL), BF16),
        scratch_shapes=[pltpu.VMEM((HALO + tm, D_POOL), F32)],
        compiler_params=_params(("arbitrary", "arbitrary")),
        name="prompt_pool",
    )(u, u, w_pool_b, pool_scale)


def _sample_pool_kernel(u_ref, state_ref, wp_ref, scale_ref, o_ref, new_state_ref, full_ref, *, batch, t):
    full_ref[:, 1:HALO, :] = state_ref[...]
    full_ref[:, HALO:, :] = u_ref[...].reshape(batch, t, D_POOL)
    for g, w in enumerate(POOL_WINDOWS):
        cols = slice(g * POOL_GROUP_W, (g + 1) * POOL_GROUP_W)
        wsum = full_ref[:, HALO:, cols]
        for s in range(1, w):
            wsum = wsum + full_ref[:, HALO - s:HALO - s + t, cols]
        d = wsum / float(w) - full_ref[:, HALO:, cols]
        y = jnp.dot(d.reshape(batch * t, POOL_GROUP_W).astype(BF16), wp_ref[g],
                    preferred_element_type=F32)
        o_ref[:, cols] = (y * scale_ref[:, cols]).astype(BF16)
    new_state_ref[...] = full_ref[:, HALO + t - POOL_HIST:, :]


def _sample_pool(u, state, w_pool_b, pool_scale, batch, t):
    full = lambda shape: pl.BlockSpec(shape, lambda i: (0,) * len(shape))
    return pl.pallas_call(
        functools.partial(_sample_pool_kernel, batch=batch, t=t),
        grid=(1,),
        in_specs=[full((batch * t, D_POOL)), full((batch, POOL_HIST, D_POOL)),
                  full((len(POOL_WINDOWS), POOL_GROUP_W, POOL_GROUP_W)), full((1, D_POOL))],
        out_specs=[full((batch * t, D_POOL)), full((batch, POOL_HIST, D_POOL))],
        out_shape=[jax.ShapeDtypeStruct((batch * t, D_POOL), BF16),
                   jax.ShapeDtypeStruct((batch, POOL_HIST, D_POOL), F32)],
        scratch_shapes=[pltpu.VMEM((batch, HALO + t, D_POOL), F32)],
        compiler_params=_params(("arbitrary",)),
        name="sample_pool",
    )(u, state, w_pool_b, pool_scale)


LN_ROWS = 128


def _layer_norm_rows(h, g, b):
    mu = jnp.mean(h, axis=-1, keepdims=True)
    c = h - mu
    var = jnp.mean(c * c, axis=-1, keepdims=True)
    return c * lax.rsqrt(var + LN_EPS) * g + b


def _out_proj_kernel(att_ref, pool_ref, x_ref, w_ref, g_ref, b_ref, o_ref, h_ref, *, n_blocks, tn):
    j = pl.program_id(1)
    acc = jnp.dot(att_ref[...], w_ref[:D_ATT, :], preferred_element_type=F32)
    acc = acc + jnp.dot(pool_ref[...], w_ref[D_ATT:, :], preferred_element_type=F32)
    h_ref[j] = DEEPNORM_ALPHA * x_ref[...] + acc

    @pl.when(j == n_blocks - 1)
    def _():
        for r in range(0, o_ref.shape[0], LN_ROWS):
            rows = slice(r, min(r + LN_ROWS, o_ref.shape[0]))
            h = jnp.concatenate([h_ref[n, rows, :] for n in range(n_blocks)], axis=-1)
            o_ref[rows, :] = _layer_norm_rows(h, g_ref[...], b_ref[...])


def _out_proj_ln(att_o, pool_o, x, w_out_b, g, b, tm, tn):
    m = x.shape[0]
    n_blocks = D_MODEL // tn
    return pl.pallas_call(
        functools.partial(_out_proj_kernel, n_blocks=n_blocks, tn=tn),
        grid=(m // tm, n_blocks),
        in_specs=[
            pl.BlockSpec((tm, D_ATT), lambda i, j: (i, 0)),
            pl.BlockSpec((tm, D_POOL), lambda i, j: (i, 0)),
            pl.BlockSpec((tm, tn), lambda i, j: (i, j)),
            pl.BlockSpec((D_MODEL, tn), lambda i, j: (0, j)),
            pl.BlockSpec((1, D_MODEL), lambda i, j: (0, 0)),
            pl.BlockSpec((1, D_MODEL), lambda i, j: (0, 0)),
        ],
        out_specs=pl.BlockSpec((tm, D_MODEL), lambda i, j: (i, 0)),
        out_shape=jax.ShapeDtypeStruct((m, D_MODEL), F32),
        scratch_shapes=[pltpu.VMEM((n_blocks, tm, tn), F32)],
        compiler_params=_params(("arbitrary", "arbitrary")),
        name="out_proj_ln",
    )(att_o, pool_o, x, w_out_b, g, b)


def _ffn_kernel(x_ref, wg_ref, wu_ref, wd_ref, g_ref, b_ref, o_ref, xb_ref, *, n_blocks):
    j = pl.program_id(1)

    @pl.when(j == 0)
    def _():
        xb_ref[...] = x_ref[...].astype(BF16)

    xb = xb_ref[...]
    gate = jnp.dot(xb, wg_ref[...], preferred_element_type=F32)
    up = jnp.dot(xb, wu_ref[...], preferred_element_type=F32)
    act = (gate * (1.0 / (1.0 + jnp.exp(-gate))) * up).astype(BF16)
    part = jnp.dot(act, wd_ref[...], preferred_element_type=F32)

    @pl.when(j == 0)
    def _():
        o_ref[...] = part

    @pl.when(j > 0)
    def _():
        o_ref[...] += part

    @pl.when(j == n_blocks - 1)
    def _():
        for r in range(0, o_ref.shape[0], LN_ROWS):
            rows = slice(r, min(r + LN_ROWS, o_ref.shape[0]))
            h = DEEPNORM_ALPHA * x_ref[rows, :] + o_ref[rows, :]
            o_ref[rows, :] = _layer_norm_rows(h, g_ref[...], b_ref[...])


def _ffn_ln(x1, wg_b, wu_b, wd_b, g, b, tm, tf):
    m = x1.shape[0]
    d_ff = wg_b.shape[1]
    n_blocks = d_ff // tf
    return pl.pallas_call(
        functools.partial(_ffn_kernel, n_blocks=n_blocks),
        grid=(m // tm, n_blocks),
        in_specs=[
            pl.BlockSpec((tm, D_MODEL), lambda i, j: (i, 0), pipeline_mode=pl.Buffered(1)),
            pl.BlockSpec((D_MODEL, tf), lambda i, j: (0, j)),
            pl.BlockSpec((D_MODEL, tf), lambda i, j: (0, j)),
            pl.BlockSpec((tf, D_MODEL), lambda i, j: (j, 0)),
            pl.BlockSpec((1, D_MODEL), lambda i, j: (0, 0)),
            pl.BlockSpec((1, D_MODEL), lambda i, j: (0, 0)),
        ],
        out_specs=pl.BlockSpec((tm, D_MODEL), lambda i, j: (i, 0)),
        out_shape=jax.ShapeDtypeStruct((m, D_MODEL), F32),
        scratch_shapes=[pltpu.VMEM((tm, D_MODEL), BF16)],
        compiler_params=_params(("arbitrary", "arbitrary")),
        name="ffn_ln",
    )(x1, wg_b, wu_b, wd_b, g, b)


def kernel(x_prompt, x_sample, cache_k, cache_v, state_pool, w_in, rel_bias, w_pool, pool_scale,
           w_out, ln1_g, ln1_b, w_gate, w_up, w_down, ln2_g, ln2_b):
    batch, seq, _ = x_prompt.shape
    dec_batch, dec_seq, _ = x_sample.shape
    depth = w_in.shape[0]
    assert depth == 1 and seq % GROUP_ROWS == 0 and seq >= ATT_WINDOW
    lc = cache_k.shape[2]
    d_ff = w_gate.shape[2]

    w_in_b = _cast_bf16(w_in[0], 128)
    w_out_b = _cast_bf16(w_out[0], 256)
    w_gate_b = _cast_bf16(w_gate[0], 128)
    w_up_b = _cast_bf16(w_up[0], 128)
    w_down_b = _cast_bf16(w_down[0], 256)
    n_groups = len(POOL_WINDOWS)
    w_pool_b = _cast_bf16(w_pool[0].reshape(n_groups * POOL_GROUP_W, POOL_GROUP_W),
                          n_groups * POOL_GROUP_W).reshape(n_groups, POOL_GROUP_W, POOL_GROUP_W)

    band_bias = _band_bias(rel_bias[0])
    sample_bias = band_bias[:, :dec_seq, :lc + dec_seq]
    g1, b1 = ln1_g[0][None], ln1_b[0][None]
    g2, b2 = ln2_g[0][None], ln2_b[0][None]
    scale = pool_scale[0][None]

    xp = x_prompt.reshape(batch * seq, D_MODEL)
    qp, kp, vp, up = _in_proj(xp, w_in_b, 1024, 512)
    att_p = _prompt_attention(qp, kp, vp, band_bias, batch, seq)
    pool_p = _prompt_pool(up, w_pool_b, scale, batch, seq, 512)
    x1p = _out_proj_ln(att_p, pool_p, xp, w_out_b, g1, b1, 512, 512)
    yp = _ffn_ln(x1p, w_gate_b, w_up_b, w_down_b, g2, b2, 512, 256)

    xs = x_sample.reshape(dec_batch * dec_seq, D_MODEL)
    qs, k_s, v_s, us = _in_proj(xs, w_in_b, dec_batch * dec_seq, 512)
    att_s = _sample_attention(qs, k_s, v_s, cache_k[0].reshape(dec_batch, lc, D_ATT),
                              cache_v[0].reshape(dec_batch, lc, D_ATT), sample_bias, dec_batch, dec_seq)
    pool_s, ps_new = _sample_pool(us, state_pool[0], w_pool_b, scale, dec_batch, dec_seq)
    x1s = _out_proj_ln(att_s, pool_s, xs, w_out_b, g1, b1, dec_batch * dec_seq, 512)
    ys = _ffn_ln(x1s, w_gate_b, w_up_b, w_down_b, g2, b2, dec_batch * dec_seq, 256)

    keep = min(ATT_WINDOW, seq)
    kp4 = kp.reshape(batch, seq, N_HEADS, HEAD_DIM)
    vp4 = vp.reshape(batch, seq, N_HEADS, HEAD_DIM)
    return (yp.reshape(batch, seq, D_MODEL),
            ys.reshape(dec_batch, dec_seq, D_MODEL),
            kp4[None, :, seq - keep:],
            vp4[None, :, seq - keep:],
            up.reshape(batch, seq, D_POOL)[None, :, seq - POOL_HIST:],
            k_s.reshape(1, dec_batch, dec_seq, N_HEADS, HEAD_DIM),
            v_s.reshape(1, dec_batch, dec_seq, N_HEADS, HEAD_DIM),
            ps_new[None])
```

```python
import functools

import jax
import jax.numpy as jnp
from jax import lax
from jax.experimental import pallas as pl
from jax.experimental.pallas import tpu as pltpu

D_MODEL = 4096
CHUNK = 64
N_PREV_CHUNKS = 8
ATT_WINDOW = N_PREV_CHUNKS * CHUNK
D_ATT = D_MODEL // 2
D_POOL = D_MODEL - D_ATT
N_HEADS = 16
HEAD_DIM = D_ATT // N_HEADS
REL_CLIP = 256
POOL_WINDOWS = (2, 4, 8, 16)
POOL_GROUP_W = D_POOL // len(POOL_WINDOWS)
POOL_HIST = max(POOL_WINDOWS) - 1
LN_EPS = 1e-5
DEEPNORM_ALPHA = 2.0 ** 0.25
NEG_INF = -1e30
ATT_SCALE = HEAD_DIM ** -0.5

GROUP_CHUNKS = 4
GROUP_ROWS = GROUP_CHUNKS * CHUNK
BAND_COLS = GROUP_ROWS + ATT_WINDOW
BIAS_PERIOD = 1024

VMEM_LIMIT_BYTES = 56 * 1024 * 1024

F32 = jnp.float32
BF16 = jnp.bfloat16


def _params(semantics):
    return pltpu.CompilerParams(dimension_semantics=semantics, vmem_limit_bytes=VMEM_LIMIT_BYTES)


def _cast_kernel(w_ref, o_ref):
    o_ref[...] = w_ref[...].astype(BF16)


def _cast_rows(w, rows):
    r, c = w.shape
    return pl.pallas_call(
        _cast_kernel,
        grid=(r // rows,),
        in_specs=[pl.BlockSpec((rows, c), lambda i: (i, 0))],
        out_specs=pl.BlockSpec((rows, c), lambda i: (i, 0)),
        out_shape=jax.ShapeDtypeStruct((r, c), BF16),
        compiler_params=_params(("arbitrary",)),
        name="cast_rows",
    )(w)


def _cast_col_blocks(w, tn, rows):
    k, n = w.shape
    return pl.pallas_call(
        _cast_kernel,
        grid=(n // tn, k // rows),
        in_specs=[pl.BlockSpec((rows, tn), lambda j, i: (i, j))],
        out_specs=pl.BlockSpec((None, rows, tn), lambda j, i: (j, i, 0)),
        out_shape=jax.ShapeDtypeStruct((n // tn, k, tn), BF16),
        compiler_params=_params(("arbitrary", "arbitrary")),
        name="cast_col_blocks",
    )(w)


def _in_proj_kernel(x_ref, w_ref, o_ref, xb_ref, *, q_blocks):
    j = pl.program_id(1)

    @pl.when(j == 0)
    def _():
        xb_ref[...] = x_ref[...].astype(BF16)

    acc = jnp.dot(xb_ref[...], w_ref[...], preferred_element_type=F32)
    o_ref[...] = acc * jnp.where(j < q_blocks, ATT_SCALE, 1.0)


def _in_proj(x, w_blocks, tm):
    m = x.shape[0]
    nb, _, tn = w_blocks.shape
    return pl.pallas_call(
        functools.partial(_in_proj_kernel, q_blocks=D_ATT // tn),
        grid=(m // tm, nb),
        in_specs=[
            pl.BlockSpec((tm, D_MODEL), lambda i, j: (i, 0), pipeline_mode=pl.Buffered(1)),
            pl.BlockSpec((None, D_MODEL, tn), lambda i, j: (j, 0, 0)),
        ],
        out_specs=pl.BlockSpec((tm, tn), lambda i, j: (i, j)),
        out_shape=jax.ShapeDtypeStruct((m, nb * tn), F32),
        scratch_shapes=[pltpu.VMEM((tm, D_MODEL), BF16)],
        compiler_params=_params(("arbitrary", "arbitrary")),
        name="in_proj",
    )(x, w_blocks)


def _bias_rows(table):
    far = table[:, 2 * REL_CLIP:]
    head = jnp.broadcast_to(far, (table.shape[0], ATT_WINDOW - REL_CLIP))
    tail = jnp.broadcast_to(far, (table.shape[0], BIAS_PERIOD - (ATT_WINDOW + REL_CLIP + 1)))
    return jnp.concatenate([head, table[:, ::-1], tail], axis=1)[:, None, :]


def _toeplitz_bias(r, rows):
    return pltpu.roll(jnp.broadcast_to(r, (rows, BIAS_PERIOD)), 0, 1, stride=1, stride_axis=0)


def _prompt_attn_kernel(r_ref, q_ref, k_ref, v_ref, o_ref, bias_ref, qb_ref, kb_ref, vb_ref, *, seq):
    @pl.when(pl.program_id(1) == 0)
    def _():
        t = _toeplitz_bias(r_ref[...], GROUP_ROWS)[:, :BAND_COLS]
        i = lax.broadcasted_iota(jnp.int32, (GROUP_ROWS, BAND_COLS), 0)
        j = lax.broadcasted_iota(jnp.int32, (GROUP_ROWS, BAND_COLS), 1)
        gap = j // CHUNK - i // CHUNK
        bias_ref[...] = jnp.where((gap >= 0) & (gap <= N_PREV_CHUNKS), t, NEG_INF)

    qb_ref[...] = q_ref[...].astype(BF16)
    kb_ref[...] = k_ref[...].astype(BF16)
    vb_ref[...] = v_ref[...].astype(BF16)
    for g in range(seq // GROUP_ROWS):
        q0 = g * GROUP_ROWS
        k0 = max(q0 - ATT_WINDOW, 0)
        nk = q0 + GROUP_ROWS - k0
        c0 = BAND_COLS - nk
        s = lax.dot_general(qb_ref[pl.ds(q0, GROUP_ROWS), :], kb_ref[pl.ds(k0, nk), :],
                            (((1,), (1,)), ((), ())), preferred_element_type=F32)
        s = s + bias_ref[:, c0:]
        m = jnp.max(s, axis=-1, keepdims=True)
        p = jnp.exp(s - m)
        l = jnp.sum(p, axis=-1, keepdims=True)
        o = jnp.dot(p.astype(BF16), vb_ref[pl.ds(k0, nk), :], preferred_element_type=F32)
        o_ref[pl.ds(q0, GROUP_ROWS), :] = (o / l).astype(BF16)


def _prompt_attention(h, bias_rows, batch, seq):
    def col(seg):
        return pl.BlockSpec((seq, HEAD_DIM), lambda hd, b: (b, seg * N_HEADS + hd))

    return pl.pallas_call(
        functools.partial(_prompt_attn_kernel, seq=seq),
        grid=(N_HEADS, batch),
        in_specs=[pl.BlockSpec((None, 1, BIAS_PERIOD), lambda hd, b: (hd, 0, 0)), col(0), col(1), col(2)],
        out_specs=pl.BlockSpec((seq, HEAD_DIM), lambda hd, b: (b, hd)),
        out_shape=jax.ShapeDtypeStruct((batch * seq, D_ATT), BF16),
        scratch_shapes=[pltpu.VMEM((GROUP_ROWS, BAND_COLS), F32)]
        + [pltpu.VMEM((seq, HEAD_DIM), BF16)] * 3,
        compiler_params=_params(("arbitrary", "arbitrary")),
        name="prompt_attention",
    )(bias_rows, h, h, h)


def _sample_attn_kernel(r_ref, q_ref, kn_ref, vn_ref, ck_ref, cv_ref, o_ref, bias_ref, *, lc, t):
    @pl.when(pl.program_id(0) == 0)
    def _():
        for h in range(N_HEADS):
            bias_ref[h] = _toeplitz_bias(r_ref[h], t)

    contract_last = (((1,), (1,)), ((), ()))
    for h in range(N_HEADS):
        cols = slice(h * HEAD_DIM, (h + 1) * HEAD_DIM)
        q = q_ref[:, cols].astype(BF16)
        kc = ck_ref[0, :, cols].astype(BF16)
        vc = cv_ref[0, :, cols].astype(BF16)
        kn = kn_ref[:, cols].astype(BF16)
        vn = vn_ref[:, cols].astype(BF16)
        s_c = lax.dot_general(q, kc, contract_last, preferred_element_type=F32) + bias_ref[h, :, :lc]
        s_n = lax.dot_general(q, kn, contract_last, preferred_element_type=F32) + bias_ref[h, :, lc:lc + t]
        m = jnp.maximum(jnp.max(s_c, axis=-1, keepdims=True), jnp.max(s_n, axis=-1, keepdims=True))
        p_c = jnp.exp(s_c - m)
        p_n = jnp.exp(s_n - m)
        l = jnp.sum(p_c, axis=-1, keepdims=True) + jnp.sum(p_n, axis=-1, keepdims=True)
        o = (jnp.dot(p_c.astype(BF16), vc, preferred_element_type=F32)
             + jnp.dot(p_n.astype(BF16), vn, preferred_element_type=F32))
        o_ref[:, cols] = (o / l).astype(BF16)


def _sample_attention(h, cache_k, cache_v, bias_rows, batch, t):
    lc = cache_k.shape[1]
    assert lc == ATT_WINDOW and t <= CHUNK
    row = lambda seg: pl.BlockSpec((t, D_ATT), lambda b: (b, seg))
    cache = lambda: pl.BlockSpec((1, lc, D_ATT), lambda b: (b, 0, 0))
    return pl.pallas_call(
        functools.partial(_sample_attn_kernel, lc=lc, t=t),
        grid=(batch,),
        in_specs=[pl.BlockSpec((N_HEADS, 1, BIAS_PERIOD), lambda b: (0, 0, 0)),
                  row(0), row(1), row(2), cache(), cache()],
        out_specs=pl.BlockSpec((t, D_ATT), lambda b: (b, 0)),
        out_shape=jax.ShapeDtypeStruct((batch * t, D_ATT), BF16),
        scratch_shapes=[pltpu.VMEM((N_HEADS, t, BIAS_PERIOD), F32)],
        compiler_params=_params(("arbitrary",)),
        name="sample_attention",
    )(bias_rows, h, h, h, cache_k, cache_v)


HALO = POOL_HIST + 1
U_SEG = 3 * D_ATT // D_POOL


def _prompt_pool_kernel(u_ref, halo_ref, wp_ref, scale_ref, o_ref, full_ref, *, tm):
    t = pl.program_id(1)

    @pl.when(t == 0)
    def _():
        full_ref[0:HALO, :] = jnp.zeros((HALO, D_POOL), F32)

    @pl.when(t > 0)
    def _():
        full_ref[0:HALO, :] = halo_ref[...]

    full_ref[HALO:, :] = u_ref[...]
    pos = t * tm + lax.broadcasted_iota(jnp.int32, (tm, 1), 0)
    for g, w in enumerate(POOL_WINDOWS):
        cols = slice(g * POOL_GROUP_W, (g + 1) * POOL_GROUP_W)
        wsum = full_ref[HALO:, cols]
        for s in range(1, w):
            wsum = wsum + full_ref[HALO - s:HALO - s + tm, cols]
        cnt = jnp.minimum(w, pos + 1).astype(F32)
        d = wsum / cnt - full_ref[HALO:, cols]
        y = jnp.dot(d.astype(BF16), wp_ref[g], preferred_element_type=F32)
        o_ref[:, cols] = (y * scale_ref[:, cols]).astype(BF16)


def _prompt_pool(h, w_pool_b, pool_scale, batch, seq, tm):
    tiles = seq // tm
    halo_per_tile = tm // HALO
    return pl.pallas_call(
        functools.partial(_prompt_pool_kernel, tm=tm),
        grid=(batch, tiles),
        in_specs=[
            pl.BlockSpec((tm, D_POOL), lambda b, t: (b * tiles + t, U_SEG)),
            pl.BlockSpec((HALO, D_POOL),
                         lambda b, t: (jnp.maximum((b * tiles + t) * halo_per_tile - 1, 0), U_SEG)),
            pl.BlockSpec((len(POOL_WINDOWS), POOL_GROUP_W, POOL_GROUP_W), lambda b, t: (0, 0, 0)),
            pl.BlockSpec((1, D_POOL), lambda b, t: (0, 0)),
        ],
        out_specs=pl.BlockSpec((tm, D_POOL), lambda b, t: (b * tiles + t, 0)),
        out_shape=jax.ShapeDtypeStruct((batch * seq, D_POOL), BF16),
        scratch_shapes=[pltpu.VMEM((HALO + tm, D_POOL), F32)],
        compiler_params=_params(("arbitrary", "arbitrary")),
        name="prompt_pool",
    )(h, h, w_pool_b, pool_scale)


def _sample_pool_kernel(u_ref, state_ref, wp_ref, scale_ref, o_ref, new_state_ref, full_ref, *, batch, t):
    full_ref[:, 1:HALO, :] = state_ref[...]
    full_ref[:, HALO:, :] = u_ref[...].reshape(batch, t, D_POOL)
    for g, w in enumerate(POOL_WINDOWS):
        cols = slice(g * POOL_GROUP_W, (g + 1) * POOL_GROUP_W)
        wsum = full_ref[:, HALO:, cols]
        for s in range(1, w):
            wsum = wsum + full_ref[:, HALO - s:HALO - s + t, cols]
        d = wsum / float(w) - full_ref[:, HALO:, cols]
        y = jnp.dot(d.reshape(batch * t, POOL_GROUP_W).astype(BF16), wp_ref[g],
                    preferred_element_type=F32)
        o_ref[:, cols] = (y * scale_ref[:, cols]).astype(BF16)
    new_state_ref[...] = full_ref[:, HALO + t - POOL_HIST:, :]


def _sample_pool(h, state, w_pool_b, pool_scale, batch, t):
    full = lambda shape: pl.BlockSpec(shape, lambda i: (0,) * len(shape))
    return pl.pallas_call(
        functools.partial(_sample_pool_kernel, batch=batch, t=t),
        grid=(1,),
        in_specs=[pl.BlockSpec((batch * t, D_POOL), lambda i: (0, U_SEG)),
                  full((batch, POOL_HIST, D_POOL)),
                  full((len(POOL_WINDOWS), POOL_GROUP_W, POOL_GROUP_W)), full((1, D_POOL))],
        out_specs=[full((batch * t, D_POOL)), full((batch, POOL_HIST, D_POOL))],
        out_shape=[jax.ShapeDtypeStruct((batch * t, D_POOL), BF16),
                   jax.ShapeDtypeStruct((batch, POOL_HIST, D_POOL), F32)],
        scratch_shapes=[pltpu.VMEM((batch, HALO + t, D_POOL), F32)],
        compiler_params=_params(("arbitrary",)),
        name="sample_pool",
    )(h, state, w_pool_b, pool_scale)


LN_ROWS = 128


def _layer_norm_rows(h, g, b):
    mu = jnp.mean(h, axis=-1, keepdims=True)
    c = h - mu
    var = jnp.mean(c * c, axis=-1, keepdims=True)
    return c * lax.rsqrt(var + LN_EPS) * g + b


def _for_row_chunks(n_rows, body):
    assert n_rows % LN_ROWS == 0

    def step(c, carry):
        body(pl.ds(pl.multiple_of(c * LN_ROWS, LN_ROWS), LN_ROWS))
        return carry

    lax.fori_loop(0, n_rows // LN_ROWS, step, 0)


def _layer_norm_in_place(o_ref, g_ref, b_ref):
    def body(rows):
        o_ref[rows, :] = _layer_norm_rows(o_ref[rows, :], g_ref[...], b_ref[...])

    _for_row_chunks(o_ref.shape[0], body)


def _out_proj_kernel(att_ref, pool_ref, x_ref, w_ref, g_ref, b_ref, o_ref, *, n_blocks, tn):
    j = pl.program_id(1)
    acc = jnp.dot(att_ref[...], w_ref[:D_ATT, :], preferred_element_type=F32)
    acc = acc + jnp.dot(pool_ref[...], w_ref[D_ATT:, :], preferred_element_type=F32)
    h = DEEPNORM_ALPHA * x_ref[...] + acc
    for n in range(n_blocks):
        @pl.when(j == n)
        def _():
            o_ref[:, n * tn:(n + 1) * tn] = h

    @pl.when(j == n_blocks - 1)
    def _():
        _layer_norm_in_place(o_ref, g_ref, b_ref)


def _out_proj_ln(att_o, pool_o, x, w_blocks, g, b, tm):
    m = x.shape[0]
    n_blocks, _, tn = w_blocks.shape
    return pl.pallas_call(
        functools.partial(_out_proj_kernel, n_blocks=n_blocks, tn=tn),
        grid=(m // tm, n_blocks),
        in_specs=[
            pl.BlockSpec((tm, D_ATT), lambda i, j: (i, 0)),
            pl.BlockSpec((tm, D_POOL), lambda i, j: (i, 0)),
            pl.BlockSpec((tm, tn), lambda i, j: (i, j)),
            pl.BlockSpec((None, D_MODEL, tn), lambda i, j: (j, 0, 0)),
            pl.BlockSpec((1, D_MODEL), lambda i, j: (0, 0)),
            pl.BlockSpec((1, D_MODEL), lambda i, j: (0, 0)),
        ],
        out_specs=pl.BlockSpec((tm, D_MODEL), lambda i, j: (i, 0), pipeline_mode=pl.Buffered(1)),
        out_shape=jax.ShapeDtypeStruct((m, D_MODEL), F32),
        compiler_params=_params(("arbitrary", "arbitrary")),
        name="out_proj_ln",
    )(att_o, pool_o, x, w_blocks, g, b)


DOWN_COLS = 512


def _ffn_kernel(x_hbm, wg_ref, wu_ref, wd_ref, g_ref, b_ref, o_ref, xb_ref, sem, *, n_blocks, tm):
    i = pl.program_id(0)
    j = pl.program_id(1)

    @pl.when(j == 0)
    def _():
        cp = pltpu.make_async_copy(x_hbm.at[pl.ds(i * tm, tm), :], o_ref, sem)
        cp.start()
        cp.wait()
        def split(rows):
            x = o_ref[rows, :]
            xb_ref[rows, :] = x.astype(BF16)
            o_ref[rows, :] = DEEPNORM_ALPHA * x

        _for_row_chunks(tm, split)

    xb = xb_ref[...]
    gate = jnp.dot(xb, wg_ref[...], preferred_element_type=F32)
    up = jnp.dot(xb, wu_ref[...], preferred_element_type=F32)
    act = (gate * (1.0 / (1.0 + jnp.exp(-gate))) * up).astype(BF16)
    for n in range(0, D_MODEL, DOWN_COLS):
        cols = slice(n, n + DOWN_COLS)
        o_ref[:, cols] += jnp.dot(act, wd_ref[:, cols], preferred_element_type=F32)

    @pl.when(j == n_blocks - 1)
    def _():
        _layer_norm_in_place(o_ref, g_ref, b_ref)


def _ffn_ln(x1, wg_blocks, wu_blocks, wd_b, g, b, tm):
    m = x1.shape[0]
    n_blocks, _, tf = wg_blocks.shape
    return pl.pallas_call(
        functools.partial(_ffn_kernel, n_blocks=n_blocks, tm=tm),
        grid=(m // tm, n_blocks),
        in_specs=[
            pl.BlockSpec(memory_space=pl.ANY),
            pl.BlockSpec((None, D_MODEL, tf), lambda i, j: (j, 0, 0)),
            pl.BlockSpec((None, D_MODEL, tf), lambda i, j: (j, 0, 0)),
            pl.BlockSpec((tf, D_MODEL), lambda i, j: (j, 0)),
            pl.BlockSpec((1, D_MODEL), lambda i, j: (0, 0)),
            pl.BlockSpec((1, D_MODEL), lambda i, j: (0, 0)),
        ],
        out_specs=pl.BlockSpec((tm, D_MODEL), lambda i, j: (i, 0), pipeline_mode=pl.Buffered(1)),
        out_shape=jax.ShapeDtypeStruct((m, D_MODEL), F32),
        scratch_shapes=[pltpu.VMEM((tm, D_MODEL), BF16), pltpu.SemaphoreType.DMA(())],
        compiler_params=_params(("arbitrary", "arbitrary")),
        name="ffn_ln",
    )(x1, wg_blocks, wu_blocks, wd_b, g, b)


IN_PROJ_COLS = 1024
OUT_PROJ_COLS = 512
FFN_COLS = 256
ROW_TILE = 1024


def kernel(x_prompt, x_sample, cache_k, cache_v, state_pool, w_in, rel_bias, w_pool, pool_scale,
           w_out, ln1_g, ln1_b, w_gate, w_up, w_down, ln2_g, ln2_b):
    batch, seq, _ = x_prompt.shape
    dec_batch, dec_seq, _ = x_sample.shape
    depth = w_in.shape[0]
    assert depth == 1 and seq % GROUP_ROWS == 0 and seq >= ATT_WINDOW
    lc = cache_k.shape[2]
    n_dec = dec_batch * dec_seq

    w_in_b = _cast_col_blocks(w_in[0], IN_PROJ_COLS, 1024)
    w_out_b = _cast_col_blocks(w_out[0], OUT_PROJ_COLS, 2048)
    w_gate_b = _cast_col_blocks(w_gate[0], FFN_COLS, D_MODEL)
    w_up_b = _cast_col_blocks(w_up[0], FFN_COLS, D_MODEL)
    w_down_b = _cast_rows(w_down[0], 256)
    n_groups = len(POOL_WINDOWS)
    w_pool_b = _cast_rows(w_pool[0].reshape(n_groups * POOL_GROUP_W, POOL_GROUP_W),
                          n_groups * POOL_GROUP_W).reshape(n_groups, POOL_GROUP_W, POOL_GROUP_W)

    bias_rows = _bias_rows(rel_bias[0])
    g1, b1 = ln1_g[0][None], ln1_b[0][None]
    g2, b2 = ln2_g[0][None], ln2_b[0][None]
    scale = pool_scale[0][None]

    xp = x_prompt.reshape(batch * seq, D_MODEL)
    hp = _in_proj(xp, w_in_b, ROW_TILE)
    att_p = _prompt_attention(hp, bias_rows, batch, seq)
    pool_p = _prompt_pool(hp, w_pool_b, scale, batch, seq, 512)
    x1p = _out_proj_ln(att_p, pool_p, xp, w_out_b, g1, b1, ROW_TILE)
    yp = _ffn_ln(x1p, w_gate_b, w_up_b, w_down_b, g2, b2, ROW_TILE)

    xs = x_sample.reshape(n_dec, D_MODEL)
    hs = _in_proj(xs, w_in_b, n_dec)
    att_s = _sample_attention(hs, cache_k[0].reshape(dec_batch, lc, D_ATT),
                              cache_v[0].reshape(dec_batch, lc, D_ATT), bias_rows, dec_batch, dec_seq)
    pool_s, ps_new = _sample_pool(hs, state_pool[0], w_pool_b, scale, dec_batch, dec_seq)
    x1s = _out_proj_ln(att_s, pool_s, xs, w_out_b, g1, b1, n_dec)
    ys = _ffn_ln(x1s, w_gate_b, w_up_b, w_down_b, g2, b2, n_dec)

    keep = min(ATT_WINDOW, seq)
    hp3 = hp.reshape(batch, seq, 4 * D_ATT)
    new_shape = (1, batch, keep, N_HEADS, HEAD_DIM)
    dec_shape = (1, dec_batch, dec_seq, N_HEADS, HEAD_DIM)
    return (yp.reshape(batch, seq, D_MODEL),
            ys.reshape(dec_batch, dec_seq, D_MODEL),
            hp3[:, seq - keep:, D_ATT:2 * D_ATT].reshape(new_shape),
            hp3[:, seq - keep:, 2 * D_ATT:3 * D_ATT].reshape(new_shape),
            hp3[None, :, seq - POOL_HIST:, 3 * D_ATT:],
            hs[:, D_ATT:2 * D_ATT].reshape(dec_shape),
            hs[:, 2 * D_ATT:3 * D_ATT].reshape(dec_shape),
            ps_new[None])
```

```python
import functools

import jax
import jax.numpy as jnp
from jax import lax
from jax.experimental import pallas as pl
from jax.experimental.pallas import tpu as pltpu

D_MODEL = 4096
CHUNK = 64
N_PREV_CHUNKS = 8
ATT_WINDOW = N_PREV_CHUNKS * CHUNK
D_ATT = D_MODEL // 2
D_POOL = D_MODEL - D_ATT
N_HEADS = 16
HEAD_DIM = D_ATT // N_HEADS
REL_CLIP = 256
POOL_WINDOWS = (2, 4, 8, 16)
POOL_GROUP_W = D_POOL // len(POOL_WINDOWS)
POOL_HIST = max(POOL_WINDOWS) - 1
LN_EPS = 1e-5
DEEPNORM_ALPHA = 2.0 ** 0.25
NEG_INF = -1e30
ATT_SCALE = HEAD_DIM ** -0.5

GROUP_CHUNKS = 4
GROUP_ROWS = GROUP_CHUNKS * CHUNK
BAND_COLS = GROUP_ROWS + ATT_WINDOW
BIAS_PERIOD = 1024

VMEM_LIMIT_BYTES = 56 * 1024 * 1024

F32 = jnp.float32
BF16 = jnp.bfloat16


def _params(semantics):
    return pltpu.CompilerParams(dimension_semantics=semantics, vmem_limit_bytes=VMEM_LIMIT_BYTES)


def _cast_kernel(w_ref, o_ref):
    o_ref[...] = w_ref[...].astype(BF16)


def _cast_rows(w, rows):
    r, c = w.shape
    return pl.pallas_call(
        _cast_kernel,
        grid=(r // rows,),
        in_specs=[pl.BlockSpec((rows, c), lambda i: (i, 0))],
        out_specs=pl.BlockSpec((rows, c), lambda i: (i, 0)),
        out_shape=jax.ShapeDtypeStruct((r, c), BF16),
        compiler_params=_params(("arbitrary",)),
        name="cast_rows",
    )(w)


def _cast_to_col_blocks(w_ref, o_ref):
    nb, _, tf = o_ref.shape
    for c in range(nb):
        o_ref[c] = w_ref[:, c * tf:(c + 1) * tf].astype(BF16)


def _project(xb_ref, w, o_ref, j, q_blocks):
    acc = jnp.dot(xb_ref[...], w, preferred_element_type=F32)
    o_ref[...] = acc * jnp.where(j < q_blocks, ATT_SCALE, 1.0)


def _in_proj_cast_kernel(x_ref, w_ref, o_ref, wb_ref, xb_ref, *, q_blocks):
    j = pl.program_id(0)

    @pl.when(j == 0)
    def _():
        xb_ref[...] = x_ref[...].astype(BF16)

    wb_ref[...] = w_ref[...].astype(BF16)
    _project(xb_ref, wb_ref[...], o_ref, j, q_blocks)


def _in_proj_cast(x, w, tn):
    m = x.shape[0]
    n = w.shape[1]
    return pl.pallas_call(
        functools.partial(_in_proj_cast_kernel, q_blocks=D_ATT // tn),
        grid=(n // tn,),
        in_specs=[
            pl.BlockSpec((m, D_MODEL), lambda j: (0, 0), pipeline_mode=pl.Buffered(1)),
            pl.BlockSpec((D_MODEL, tn), lambda j: (0, j)),
        ],
        out_specs=[pl.BlockSpec((m, tn), lambda j: (0, j)),
                   pl.BlockSpec((None, D_MODEL, tn), lambda j: (j, 0, 0))],
        out_shape=[jax.ShapeDtypeStruct((m, n), F32),
                   jax.ShapeDtypeStruct((n // tn, D_MODEL, tn), BF16)],
        scratch_shapes=[pltpu.VMEM((m, D_MODEL), BF16)],
        compiler_params=_params(("arbitrary",)),
        name="in_proj_cast",
    )(x, w)


def _in_proj_kernel(x_ref, w_ref, wg_ref, wu_ref, o_ref, wgb_ref, wub_ref, xb_ref, *, q_blocks):
    j = pl.program_id(1)

    @pl.when(j == 0)
    def _():
        xb_ref[...] = x_ref[...].astype(BF16)

    _cast_to_col_blocks(wg_ref, wgb_ref)
    _cast_to_col_blocks(wu_ref, wub_ref)
    _project(xb_ref, w_ref[...], o_ref, j, q_blocks)


def _in_proj(x, w_blocks, w_gate, w_up, tm, tf):
    m = x.shape[0]
    nb, _, tn = w_blocks.shape
    d_ff = w_gate.shape[1]
    steps = (m // tm) * nb
    slab = D_MODEL // steps
    assert slab * steps == D_MODEL and slab % 16 == 0 and d_ff % tf == 0
    side_in = lambda: pl.BlockSpec((slab, d_ff), lambda i, j: (i * nb + j, 0))
    side_out = lambda: pl.BlockSpec((d_ff // tf, slab, tf), lambda i, j: (0, i * nb + j, 0))
    side_shape = jax.ShapeDtypeStruct((d_ff // tf, D_MODEL, tf), BF16)
    return pl.pallas_call(
        functools.partial(_in_proj_kernel, q_blocks=D_ATT // tn),
        grid=(m // tm, nb),
        in_specs=[
            pl.BlockSpec((tm, D_MODEL), lambda i, j: (i, 0), pipeline_mode=pl.Buffered(1)),
            pl.BlockSpec((None, D_MODEL, tn), lambda i, j: (j, 0, 0)),
            side_in(), side_in(),
        ],
        out_specs=[pl.BlockSpec((tm, tn), lambda i, j: (i, j)), side_out(), side_out()],
        out_shape=[jax.ShapeDtypeStruct((m, nb * tn), F32), side_shape, side_shape],
        scratch_shapes=[pltpu.VMEM((tm, D_MODEL), BF16)],
        compiler_params=_params(("arbitrary", "arbitrary")),
        name="in_proj",
    )(x, w_blocks, w_gate, w_up)


def _bias_rows(table):
    far = table[:, 2 * REL_CLIP:]
    head = jnp.broadcast_to(far, (table.shape[0], ATT_WINDOW - REL_CLIP))
    tail = jnp.broadcast_to(far, (table.shape[0], BIAS_PERIOD - (ATT_WINDOW + REL_CLIP + 1)))
    return jnp.concatenate([head, table[:, ::-1], tail], axis=1)[:, None, :]


def _toeplitz_bias(r, rows):
    return pltpu.roll(jnp.broadcast_to(r, (rows, BIAS_PERIOD)), 0, 1, stride=1, stride_axis=0)


def _prompt_attn_kernel(r_ref, q_ref, k_ref, v_ref, wd_ref, o_ref, wdb_ref, bias_ref, qb_ref, kb_ref, vb_ref,
                        *, seq, wd_blocks):
    @pl.when(pl.program_id(0) * pl.num_programs(1) + pl.program_id(1) < wd_blocks)
    def _():
        wdb_ref[...] = wd_ref[...].astype(BF16)

    @pl.when(pl.program_id(1) == 0)
    def _():
        t = _toeplitz_bias(r_ref[...], GROUP_ROWS)[:, :BAND_COLS]
        i = lax.broadcasted_iota(jnp.int32, (GROUP_ROWS, BAND_COLS), 0)
        j = lax.broadcasted_iota(jnp.int32, (GROUP_ROWS, BAND_COLS), 1)
        gap = j // CHUNK - i // CHUNK
        bias_ref[...] = jnp.where((gap >= 0) & (gap <= N_PREV_CHUNKS), t, NEG_INF)

    qb_ref[...] = q_ref[...].astype(BF16)
    kb_ref[...] = k_ref[...].astype(BF16)
    vb_ref[...] = v_ref[...].astype(BF16)
    for g in range(seq // GROUP_ROWS):
        q0 = g * GROUP_ROWS
        k0 = max(q0 - ATT_WINDOW, 0)
        nk = q0 + GROUP_ROWS - k0
        c0 = BAND_COLS - nk
        s = lax.dot_general(qb_ref[pl.ds(q0, GROUP_ROWS), :], kb_ref[pl.ds(k0, nk), :],
                            (((1,), (1,)), ((), ())), preferred_element_type=F32)
        s = s + bias_ref[:, c0:]
        m = jnp.max(s, axis=-1, keepdims=True)
        p = jnp.exp(s - m)
        l = jnp.sum(p, axis=-1, keepdims=True)
        o = jnp.dot(p.astype(BF16), vb_ref[pl.ds(k0, nk), :], preferred_element_type=F32)
        o_ref[pl.ds(q0, GROUP_ROWS), :] = (o / l).astype(BF16)


def _prompt_attention(h, bias_rows, w_down, batch, seq, wd_rows):
    d_ff = w_down.shape[0]
    wd_blocks = d_ff // wd_rows
    assert wd_blocks * wd_rows == d_ff and wd_blocks <= N_HEADS * batch

    def col(seg):
        return pl.BlockSpec((seq, HEAD_DIM), lambda hd, b: (b, seg * N_HEADS + hd))

    wd_spec = lambda: pl.BlockSpec((wd_rows, D_MODEL),
                                   lambda hd, b: (jnp.minimum(hd * batch + b, wd_blocks - 1), 0))
    return pl.pallas_call(
        functools.partial(_prompt_attn_kernel, seq=seq, wd_blocks=wd_blocks),
        grid=(N_HEADS, batch),
        in_specs=[pl.BlockSpec((None, 1, BIAS_PERIOD), lambda hd, b: (hd, 0, 0)), col(0), col(1), col(2),
                  wd_spec()],
        out_specs=[pl.BlockSpec((seq, HEAD_DIM), lambda hd, b: (b, hd)), wd_spec()],
        out_shape=[jax.ShapeDtypeStruct((batch * seq, D_ATT), BF16),
                   jax.ShapeDtypeStruct((d_ff, D_MODEL), BF16)],
        scratch_shapes=[pltpu.VMEM((GROUP_ROWS, BAND_COLS), F32)]
        + [pltpu.VMEM((seq, HEAD_DIM), BF16)] * 3,
        compiler_params=_params(("arbitrary", "arbitrary")),
        name="prompt_attention",
    )(bias_rows, h, h, h, w_down)


def _sample_attn_kernel(r_ref, q_ref, kn_ref, vn_ref, ck_ref, cv_ref, o_ref, bias_ref, *, lc, t):
    @pl.when(pl.program_id(0) == 0)
    def _():
        for h in range(N_HEADS):
            bias_ref[h] = _toeplitz_bias(r_ref[h], t)

    contract_last = (((1,), (1,)), ((), ()))
    for h in range(N_HEADS):
        cols = slice(h * HEAD_DIM, (h + 1) * HEAD_DIM)
        q = q_ref[:, cols].astype(BF16)
        kc = ck_ref[0, :, cols].astype(BF16)
        vc = cv_ref[0, :, cols].astype(BF16)
        kn = kn_ref[:, cols].astype(BF16)
        vn = vn_ref[:, cols].astype(BF16)
        s_c = lax.dot_general(q, kc, contract_last, preferred_element_type=F32) + bias_ref[h, :, :lc]
        s_n = lax.dot_general(q, kn, contract_last, preferred_element_type=F32) + bias_ref[h, :, lc:lc + t]
        m = jnp.maximum(jnp.max(s_c, axis=-1, keepdims=True), jnp.max(s_n, axis=-1, keepdims=True))
        p_c = jnp.exp(s_c - m)
        p_n = jnp.exp(s_n - m)
        l = jnp.sum(p_c, axis=-1, keepdims=True) + jnp.sum(p_n, axis=-1, keepdims=True)
        o = (jnp.dot(p_c.astype(BF16), vc, preferred_element_type=F32)
             + jnp.dot(p_n.astype(BF16), vn, preferred_element_type=F32))
        o_ref[:, cols] = (o / l).astype(BF16)


def _sample_attention(h, cache_k, cache_v, bias_rows, batch, t):
    lc = cache_k.shape[1]
    assert lc == ATT_WINDOW and t <= CHUNK
    row = lambda seg: pl.BlockSpec((t, D_ATT), lambda b: (b, seg))
    cache = lambda: pl.BlockSpec((1, lc, D_ATT), lambda b: (b, 0, 0))
    return pl.pallas_call(
        functools.partial(_sample_attn_kernel, lc=lc, t=t),
        grid=(batch,),
        in_specs=[pl.BlockSpec((N_HEADS, 1, BIAS_PERIOD), lambda b: (0, 0, 0)),
                  row(0), row(1), row(2), cache(), cache()],
        out_specs=pl.BlockSpec((t, D_ATT), lambda b: (b, 0)),
        out_shape=jax.ShapeDtypeStruct((batch * t, D_ATT), BF16),
        scratch_shapes=[pltpu.VMEM((N_HEADS, t, BIAS_PERIOD), F32)],
        compiler_params=_params(("arbitrary",)),
        name="sample_attention",
    )(bias_rows, h, h, h, cache_k, cache_v)


HALO = POOL_HIST + 1
U_SEG = 3 * D_ATT // D_POOL


def _prompt_pool_kernel(u_ref, halo_ref, wp_ref, scale_ref, o_ref, full_ref, *, tm):
    t = pl.program_id(1)

    @pl.when(t == 0)
    def _():
        full_ref[0:HALO, :] = jnp.zeros((HALO, D_POOL), F32)

    @pl.when(t > 0)
    def _():
        full_ref[0:HALO, :] = halo_ref[...]

    full_ref[HALO:, :] = u_ref[...]
    pos = t * tm + lax.broadcasted_iota(jnp.int32, (tm, 1), 0)
    for g, w in enumerate(POOL_WINDOWS):
        cols = slice(g * POOL_GROUP_W, (g + 1) * POOL_GROUP_W)
        wsum = full_ref[HALO:, cols]
        for s in range(1, w):
            wsum = wsum + full_ref[HALO - s:HALO - s + tm, cols]
        cnt = jnp.minimum(w, pos + 1).astype(F32)
        d = wsum / cnt - full_ref[HALO:, cols]
        y = jnp.dot(d.astype(BF16), wp_ref[g], preferred_element_type=F32)
        o_ref[:, cols] = (y * scale_ref[:, cols]).astype(BF16)


def _prompt_pool(h, w_pool_b, pool_scale, batch, seq, tm):
    tiles = seq // tm
    halo_per_tile = tm // HALO
    return pl.pallas_call(
        functools.partial(_prompt_pool_kernel, tm=tm),
        grid=(batch, tiles),
        in_specs=[
            pl.BlockSpec((tm, D_POOL), lambda b, t: (b * tiles + t, U_SEG)),
            pl.BlockSpec((HALO, D_POOL),
                         lambda b, t: (jnp.maximum((b * tiles + t) * halo_per_tile - 1, 0), U_SEG)),
            pl.BlockSpec((len(POOL_WINDOWS), POOL_GROUP_W, POOL_GROUP_W), lambda b, t: (0, 0, 0)),
            pl.BlockSpec((1, D_POOL), lambda b, t: (0, 0)),
        ],
        out_specs=pl.BlockSpec((tm, D_POOL), lambda b, t: (b * tiles + t, 0)),
        out_shape=jax.ShapeDtypeStruct((batch * seq, D_POOL), BF16),
        scratch_shapes=[pltpu.VMEM((HALO + tm, D_POOL), F32)],
        compiler_params=_params(("arbitrary", "arbitrary")),
        name="prompt_pool",
    )(h, h, w_pool_b, pool_scale)


def _sample_pool_kernel(u_ref, state_ref, wp_ref, scale_ref, o_ref, new_state_ref, full_ref, *, batch, t):
    full_ref[:, 1:HALO, :] = state_ref[...]
    full_ref[:, HALO:, :] = u_ref[...].reshape(batch, t, D_POOL)
    for g, w in enumerate(POOL_WINDOWS):
        cols = slice(g * POOL_GROUP_W, (g + 1) * POOL_GROUP_W)
        wsum = full_ref[:, HALO:, cols]
        for s in range(1, w):
            wsum = wsum + full_ref[:, HALO - s:HALO - s + t, cols]
        d = wsum / float(w) - full_ref[:, HALO:, cols]
        y = jnp.dot(d.reshape(batch * t, POOL_GROUP_W).astype(BF16), wp_ref[g],
                    preferred_element_type=F32)
        o_ref[:, cols] = (y * scale_ref[:, cols]).astype(BF16)
    new_state_ref[...] = full_ref[:, HALO + t - POOL_HIST:, :]


def _sample_pool(h, state, w_pool_b, pool_scale, batch, t):
    full = lambda shape: pl.BlockSpec(shape, lambda i: (0,) * len(shape))
    return pl.pallas_call(
        functools.partial(_sample_pool_kernel, batch=batch, t=t),
        grid=(1,),
        in_specs=[pl.BlockSpec((batch * t, D_POOL), lambda i: (0, U_SEG)),
                  full((batch, POOL_HIST, D_POOL)),
                  full((len(POOL_WINDOWS), POOL_GROUP_W, POOL_GROUP_W)), full((1, D_POOL))],
        out_specs=[full((batch * t, D_POOL)), full((batch, POOL_HIST, D_POOL))],
        out_shape=[jax.ShapeDtypeStruct((batch * t, D_POOL), BF16),
                   jax.ShapeDtypeStruct((batch, POOL_HIST, D_POOL), F32)],
        scratch_shapes=[pltpu.VMEM((batch, HALO + t, D_POOL), F32)],
        compiler_params=_params(("arbitrary",)),
        name="sample_pool",
    )(h, state, w_pool_b, pool_scale)


LN_ROWS = 128


def _layer_norm_rows(h, g, b):
    mu = jnp.mean(h, axis=-1, keepdims=True)
    c = h - mu
    var = jnp.mean(c * c, axis=-1, keepdims=True)
    return c * lax.rsqrt(var + LN_EPS) * g + b


def _for_row_chunks(n_rows, body):
    assert n_rows % LN_ROWS == 0

    def step(c, carry):
        body(pl.ds(pl.multiple_of(c * LN_ROWS, LN_ROWS), LN_ROWS))
        return carry

    lax.fori_loop(0, n_rows // LN_ROWS, step, 0)


def _layer_norm_in_place(o_ref, g_ref, b_ref):
    def body(rows):
        o_ref[rows, :] = _layer_norm_rows(o_ref[rows, :], g_ref[...], b_ref[...])

    _for_row_chunks(o_ref.shape[0], body)


def _out_proj_kernel(att_ref, pool_ref, x_ref, w_ref, g_ref, b_ref, o_ref, *maybe_wb_ref, n_blocks, tn):
    j = pl.program_id(1)
    if maybe_wb_ref:
        wb_ref, = maybe_wb_ref
        wb_ref[...] = w_ref[...].astype(BF16)
    else:
        wb_ref = w_ref
    acc = jnp.dot(att_ref[...], wb_ref[:D_ATT, :], preferred_element_type=F32)
    acc = acc + jnp.dot(pool_ref[...], wb_ref[D_ATT:, :], preferred_element_type=F32)
    h = DEEPNORM_ALPHA * x_ref[...] + acc
    for n in range(n_blocks):
        @pl.when(j == n)
        def _():
            o_ref[:, n * tn:(n + 1) * tn] = h

    @pl.when(j == n_blocks - 1)
    def _():
        _layer_norm_in_place(o_ref, g_ref, b_ref)


def _out_proj_ln(att_o, pool_o, x, w, g, b, tm, tn):
    m = x.shape[0]
    n_blocks = D_MODEL // tn
    cast_weights = w.dtype == F32
    blocked = pl.BlockSpec((None, D_MODEL, tn), lambda i, j: (j, 0, 0))
    out_specs = [pl.BlockSpec((tm, D_MODEL), lambda i, j: (i, 0), pipeline_mode=pl.Buffered(1))]
    out_shape = [jax.ShapeDtypeStruct((m, D_MODEL), F32)]
    if cast_weights:
        assert m == tm
        w_spec = pl.BlockSpec((D_MODEL, tn), lambda i, j: (0, j))
        out_specs.append(blocked)
        out_shape.append(jax.ShapeDtypeStruct((n_blocks, D_MODEL, tn), BF16))
    else:
        w_spec = blocked
    return pl.pallas_call(
        functools.partial(_out_proj_kernel, n_blocks=n_blocks, tn=tn),
        grid=(m // tm, n_blocks),
        in_specs=[
            pl.BlockSpec((tm, D_ATT), lambda i, j: (i, 0)),
            pl.BlockSpec((tm, D_POOL), lambda i, j: (i, 0)),
            pl.BlockSpec((tm, tn), lambda i, j: (i, j)),
            w_spec,
            pl.BlockSpec((1, D_MODEL), lambda i, j: (0, 0)),
            pl.BlockSpec((1, D_MODEL), lambda i, j: (0, 0)),
        ],
        out_specs=out_specs,
        out_shape=out_shape,
        compiler_params=_params(("arbitrary", "arbitrary")),
        name="out_proj_ln",
    )(att_o, pool_o, x, w, g, b)


DOWN_COLS = 512


def _ffn_kernel(x_hbm, wg_ref, wu_ref, wd_ref, g_ref, b_ref, o_ref, xb_ref, sem, *, n_blocks, tm):
    i = pl.program_id(0)
    j = pl.program_id(1)

    @pl.when(j == 0)
    def _():
        cp = pltpu.make_async_copy(x_hbm.at[pl.ds(i * tm, tm), :], o_ref, sem)
        cp.start()
        cp.wait()
        def split(rows):
            x = o_ref[rows, :]
            xb_ref[rows, :] = x.astype(BF16)
            o_ref[rows, :] = DEEPNORM_ALPHA * x

        _for_row_chunks(tm, split)

    xb = xb_ref[...]
    gate = jnp.dot(xb, wg_ref[...], preferred_element_type=F32)
    up = jnp.dot(xb, wu_ref[...], preferred_element_type=F32)
    act = (gate * (1.0 / (1.0 + jnp.exp(-gate))) * up).astype(BF16)
    for n in range(0, D_MODEL, DOWN_COLS):
        cols = slice(n, n + DOWN_COLS)
        o_ref[:, cols] += jnp.dot(act, wd_ref[:, cols], preferred_element_type=F32)

    @pl.when(j == n_blocks - 1)
    def _():
        _layer_norm_in_place(o_ref, g_ref, b_ref)


def _ffn_ln(x1, wg_blocks, wu_blocks, wd_b, g, b, tm):
    m = x1.shape[0]
    n_blocks, _, tf = wg_blocks.shape
    return pl.pallas_call(
        functools.partial(_ffn_kernel, n_blocks=n_blocks, tm=tm),
        grid=(m // tm, n_blocks),
        in_specs=[
            pl.BlockSpec(memory_space=pl.ANY),
            pl.BlockSpec((None, D_MODEL, tf), lambda i, j: (j, 0, 0)),
            pl.BlockSpec((None, D_MODEL, tf), lambda i, j: (j, 0, 0)),
            pl.BlockSpec((tf, D_MODEL), lambda i, j: (j, 0)),
            pl.BlockSpec((1, D_MODEL), lambda i, j: (0, 0)),
            pl.BlockSpec((1, D_MODEL), lambda i, j: (0, 0)),
        ],
        out_specs=pl.BlockSpec((tm, D_MODEL), lambda i, j: (i, 0), pipeline_mode=pl.Buffered(1)),
        out_shape=jax.ShapeDtypeStruct((m, D_MODEL), F32),
        scratch_shapes=[pltpu.VMEM((tm, D_MODEL), BF16), pltpu.SemaphoreType.DMA(())],
        compiler_params=_params(("arbitrary", "arbitrary")),
        name="ffn_ln",
    )(x1, wg_blocks, wu_blocks, wd_b, g, b)


IN_PROJ_COLS = 512
OUT_PROJ_COLS = 512
FFN_COLS = 256
ROW_TILE = 1024
W_DOWN_CAST_ROWS = 256


def kernel(x_prompt, x_sample, cache_k, cache_v, state_pool, w_in, rel_bias, w_pool, pool_scale,
           w_out, ln1_g, ln1_b, w_gate, w_up, w_down, ln2_g, ln2_b):
    batch, seq, _ = x_prompt.shape
    dec_batch, dec_seq, _ = x_sample.shape
    depth = w_in.shape[0]
    assert depth == 1 and seq % GROUP_ROWS == 0 and seq >= ATT_WINDOW
    lc = cache_k.shape[2]
    n_dec = dec_batch * dec_seq

    n_groups = len(POOL_WINDOWS)
    w_pool_b = _cast_rows(w_pool[0].reshape(n_groups * POOL_GROUP_W, POOL_GROUP_W),
                          n_groups * POOL_GROUP_W).reshape(n_groups, POOL_GROUP_W, POOL_GROUP_W)
    bias_rows = _bias_rows(rel_bias[0])
    g1, b1 = ln1_g[0][None], ln1_b[0][None]
    g2, b2 = ln2_g[0][None], ln2_b[0][None]
    scale = pool_scale[0][None]
    xp = x_prompt.reshape(batch * seq, D_MODEL)
    xs = x_sample.reshape(n_dec, D_MODEL)

    hs, w_in_b = _in_proj_cast(xs, w_in[0], IN_PROJ_COLS)
    hp, w_gate_b, w_up_b = _in_proj(xp, w_in_b, w_gate[0], w_up[0], ROW_TILE, FFN_COLS)

    att_s = _sample_attention(hs, cache_k[0].reshape(dec_batch, lc, D_ATT),
                              cache_v[0].reshape(dec_batch, lc, D_ATT), bias_rows, dec_batch, dec_seq)
    pool_s, ps_new = _sample_pool(hs, state_pool[0], w_pool_b, scale, dec_batch, dec_seq)
    x1s, w_out_b = _out_proj_ln(att_s, pool_s, xs, w_out[0], g1, b1, n_dec, OUT_PROJ_COLS)

    att_p, w_down_b = _prompt_attention(hp, bias_rows, w_down[0], batch, seq, W_DOWN_CAST_ROWS)
    pool_p = _prompt_pool(hp, w_pool_b, scale, batch, seq, 512)
    x1p, = _out_proj_ln(att_p, pool_p, xp, w_out_b, g1, b1, ROW_TILE, OUT_PROJ_COLS)

    ys = _ffn_ln(x1s, w_gate_b, w_up_b, w_down_b, g2, b2, n_dec)
    yp = _ffn_ln(x1p, w_gate_b, w_up_b, w_down_b, g2, b2, ROW_TILE)

    keep = min(ATT_WINDOW, seq)
    hp3 = hp.reshape(batch, seq, 4 * D_ATT)
    new_shape = (1, batch, keep, N_HEADS, HEAD_DIM)
    dec_shape = (1, dec_batch, dec_seq, N_HEADS, HEAD_DIM)
    return (yp.reshape(batch, seq, D_MODEL),
            ys.reshape(dec_batch, dec_seq, D_MODEL),
            hp3[:, seq - keep:, D_ATT:2 * D_ATT].reshape(new_shape),
            hp3[:, seq - keep:, 2 * D_ATT:3 * D_ATT].reshape(new_shape),
            hp3[None, :, seq - POOL_HIST:, 3 * D_ATT:],
            hs[:, D_ATT:2 * D_ATT].reshape(dec_shape),
            hs[:, 2 * D_ATT:3 * D_ATT].reshape(dec_shape),
            ps_new[None])
```

```python
import functools

import jax
import jax.numpy as jnp
from jax import lax
from jax.experimental import pallas as pl
from jax.experimental.pallas import tpu as pltpu

D_MODEL = 4096
CHUNK = 64
N_PREV_CHUNKS = 8
ATT_WINDOW = N_PREV_CHUNKS * CHUNK
D_ATT = D_MODEL // 2
D_POOL = D_MODEL - D_ATT
N_HEADS = 16
HEAD_DIM = D_ATT // N_HEADS
REL_CLIP = 256
POOL_WINDOWS = (2, 4, 8, 16)
POOL_GROUP_W = D_POOL // len(POOL_WINDOWS)
POOL_HIST = max(POOL_WINDOWS) - 1
LN_EPS = 1e-5
DEEPNORM_ALPHA = 2.0 ** 0.25
NEG_INF = -1e30
ATT_SCALE = HEAD_DIM ** -0.5

GROUP_CHUNKS = 4
GROUP_ROWS = GROUP_CHUNKS * CHUNK
BAND_COLS = GROUP_ROWS + ATT_WINDOW
BIAS_PERIOD = 1024

VMEM_LIMIT_BYTES = 56 * 1024 * 1024

F32 = jnp.float32
BF16 = jnp.bfloat16


def _params(semantics):
    return pltpu.CompilerParams(dimension_semantics=semantics, vmem_limit_bytes=VMEM_LIMIT_BYTES)


def _cast_kernel(w_ref, o_ref):
    o_ref[...] = w_ref[...].astype(BF16)


def _cast_rows(w, rows):
    r, c = w.shape
    return pl.pallas_call(
        _cast_kernel,
        grid=(r // rows,),
        in_specs=[pl.BlockSpec((rows, c), lambda i: (i, 0))],
        out_specs=pl.BlockSpec((rows, c), lambda i: (i, 0)),
        out_shape=jax.ShapeDtypeStruct((r, c), BF16),
        compiler_params=_params(("arbitrary",)),
        name="cast_rows",
    )(w)


def _cast_to_col_blocks(w_ref, o_ref):
    nb, _, tf = o_ref.shape
    for c in range(nb):
        o_ref[c] = w_ref[:, c * tf:(c + 1) * tf].astype(BF16)


def _project(xb_ref, w, o_ref, j, q_blocks):
    acc = jnp.dot(xb_ref[...], w, preferred_element_type=F32)
    o_ref[...] = acc * jnp.where(j < q_blocks, ATT_SCALE, 1.0)


def _in_proj_cast_kernel(x_ref, w_ref, o_ref, wb_ref, xb_ref, *, q_blocks):
    j = pl.program_id(0)

    @pl.when(j == 0)
    def _():
        xb_ref[...] = x_ref[...].astype(BF16)

    wb_ref[...] = w_ref[...].astype(BF16)
    _project(xb_ref, wb_ref[...], o_ref, j, q_blocks)


def _in_proj_cast(x, w, tn):
    m = x.shape[0]
    n = w.shape[1]
    return pl.pallas_call(
        functools.partial(_in_proj_cast_kernel, q_blocks=D_ATT // tn),
        grid=(n // tn,),
        in_specs=[
            pl.BlockSpec((m, D_MODEL), lambda j: (0, 0), pipeline_mode=pl.Buffered(1)),
            pl.BlockSpec((D_MODEL, tn), lambda j: (0, j)),
        ],
        out_specs=[pl.BlockSpec((m, tn), lambda j: (0, j)),
                   pl.BlockSpec((None, D_MODEL, tn), lambda j: (j, 0, 0))],
        out_shape=[jax.ShapeDtypeStruct((m, n), F32),
                   jax.ShapeDtypeStruct((n // tn, D_MODEL, tn), BF16)],
        scratch_shapes=[pltpu.VMEM((m, D_MODEL), BF16)],
        compiler_params=_params(("arbitrary",)),
        name="in_proj_cast",
    )(x, w)


def _in_proj_kernel(x_ref, w_ref, wg_ref, wu_ref, o_ref, wgb_ref, wub_ref, xb_ref, *, q_blocks):
    j = pl.program_id(1)

    @pl.when(j == 0)
    def _():
        xb_ref[...] = x_ref[...].astype(BF16)

    _cast_to_col_blocks(wg_ref, wgb_ref)
    _cast_to_col_blocks(wu_ref, wub_ref)
    _project(xb_ref, w_ref[...], o_ref, j, q_blocks)


def _in_proj(x, w_blocks, w_gate, w_up, tm, tf):
    m = x.shape[0]
    nb, _, tn = w_blocks.shape
    d_ff = w_gate.shape[1]
    steps = (m // tm) * nb
    slab = D_MODEL // steps
    assert slab * steps == D_MODEL and slab % 16 == 0 and d_ff % tf == 0
    side_in = lambda: pl.BlockSpec((slab, d_ff), lambda i, j: (i * nb + j, 0))
    side_out = lambda: pl.BlockSpec((d_ff // tf, slab, tf), lambda i, j: (0, i * nb + j, 0))
    side_shape = jax.ShapeDtypeStruct((d_ff // tf, D_MODEL, tf), BF16)
    return pl.pallas_call(
        functools.partial(_in_proj_kernel, q_blocks=D_ATT // tn),
        grid=(m // tm, nb),
        in_specs=[
            pl.BlockSpec((tm, D_MODEL), lambda i, j: (i, 0), pipeline_mode=pl.Buffered(1)),
            pl.BlockSpec((None, D_MODEL, tn), lambda i, j: (j, 0, 0)),
            side_in(), side_in(),
        ],
        out_specs=[pl.BlockSpec((tm, tn), lambda i, j: (i, j)), side_out(), side_out()],
        out_shape=[jax.ShapeDtypeStruct((m, nb * tn), F32), side_shape, side_shape],
        scratch_shapes=[pltpu.VMEM((tm, D_MODEL), BF16)],
        compiler_params=_params(("arbitrary", "arbitrary")),
        name="in_proj",
    )(x, w_blocks, w_gate, w_up)


def _bias_rows(table):
    far = table[:, 2 * REL_CLIP:]
    head = jnp.broadcast_to(far, (table.shape[0], ATT_WINDOW - REL_CLIP))
    tail = jnp.broadcast_to(far, (table.shape[0], BIAS_PERIOD - (ATT_WINDOW + REL_CLIP + 1)))
    return jnp.concatenate([head, table[:, ::-1], tail], axis=1)[:, None, :]


def _toeplitz_bias(r, rows):
    return pltpu.roll(jnp.broadcast_to(r, (rows, BIAS_PERIOD)), 0, 1, stride=1, stride_axis=0)


def _prompt_attn_kernel(r_ref, q_ref, k_ref, v_ref, wd_ref, o_ref, wdb_ref, kn_hbm, vn_hbm,
                        bias_ref, qb_ref, kb_ref, vb_ref, new_sem, *, seq, wd_blocks):
    keep = kn_hbm.shape[2]
    hd, b = pl.program_id(0), pl.program_id(1)
    new_copies = [
        pltpu.make_async_copy(src.at[pl.ds(seq - keep, keep), :], dst.at[0, b, :, hd, :], new_sem.at[n])
        for n, (src, dst) in enumerate(((k_ref, kn_hbm), (v_ref, vn_hbm)))]
    for cp in new_copies:
        cp.start()

    @pl.when(pl.program_id(0) * pl.num_programs(1) + pl.program_id(1) < wd_blocks)
    def _():
        wdb_ref[...] = wd_ref[...].astype(BF16)

    @pl.when(pl.program_id(1) == 0)
    def _():
        t = _toeplitz_bias(r_ref[...], GROUP_ROWS)[:, :BAND_COLS]
        i = lax.broadcasted_iota(jnp.int32, (GROUP_ROWS, BAND_COLS), 0)
        j = lax.broadcasted_iota(jnp.int32, (GROUP_ROWS, BAND_COLS), 1)
        gap = j // CHUNK - i // CHUNK
        bias_ref[...] = jnp.where((gap >= 0) & (gap <= N_PREV_CHUNKS), t, NEG_INF)

    qb_ref[...] = q_ref[...].astype(BF16)
    kb_ref[...] = k_ref[...].astype(BF16)
    vb_ref[...] = v_ref[...].astype(BF16)
    for g in range(seq // GROUP_ROWS):
        q0 = g * GROUP_ROWS
        k0 = max(q0 - ATT_WINDOW, 0)
        nk = q0 + GROUP_ROWS - k0
        c0 = BAND_COLS - nk
        s = lax.dot_general(qb_ref[pl.ds(q0, GROUP_ROWS), :], kb_ref[pl.ds(k0, nk), :],
                            (((1,), (1,)), ((), ())), preferred_element_type=F32)
        s = s + bias_ref[:, c0:]
        m = jnp.max(s, axis=-1, keepdims=True)
        p = jnp.exp(s - m)
        l = jnp.sum(p, axis=-1, keepdims=True)
        o = jnp.dot(p.astype(BF16), vb_ref[pl.ds(k0, nk), :], preferred_element_type=F32)
        o_ref[pl.ds(q0, GROUP_ROWS), :] = (o / l).astype(BF16)
    for cp in new_copies:
        cp.wait()


def _prompt_attention(h, bias_rows, w_down, batch, seq, wd_rows):
    keep = min(ATT_WINDOW, seq)
    new_rows = lambda: pl.BlockSpec(memory_space=pl.ANY)
    new_shape = jax.ShapeDtypeStruct((1, batch, keep, N_HEADS, HEAD_DIM), F32)
    d_ff = w_down.shape[0]
    wd_blocks = d_ff // wd_rows
    assert wd_blocks * wd_rows == d_ff and wd_blocks <= N_HEADS * batch

    def col(seg):
        return pl.BlockSpec((seq, HEAD_DIM), lambda hd, b: (b, seg * N_HEADS + hd))

    wd_spec = lambda: pl.BlockSpec((wd_rows, D_MODEL),
                                   lambda hd, b: (jnp.minimum(hd * batch + b, wd_blocks - 1), 0))
    return pl.pallas_call(
        functools.partial(_prompt_attn_kernel, seq=seq, wd_blocks=wd_blocks),
        grid=(N_HEADS, batch),
        in_specs=[pl.BlockSpec((None, 1, BIAS_PERIOD), lambda hd, b: (hd, 0, 0)), col(0), col(1), col(2),
                  wd_spec()],
        out_specs=[pl.BlockSpec((seq, HEAD_DIM), lambda hd, b: (b, hd)), wd_spec(), new_rows(), new_rows()],
        out_shape=[jax.ShapeDtypeStruct((batch * seq, D_ATT), BF16),
                   jax.ShapeDtypeStruct((d_ff, D_MODEL), BF16), new_shape, new_shape],
        scratch_shapes=[pltpu.VMEM((GROUP_ROWS, BAND_COLS), F32)]
        + [pltpu.VMEM((seq, HEAD_DIM), BF16)] * 3 + [pltpu.SemaphoreType.DMA((2,))],
        compiler_params=_params(("arbitrary", "arbitrary")),
        name="prompt_attention",
    )(bias_rows, h, h, h, w_down)


def _sample_attn_kernel(r_ref, q_ref, kn_ref, vn_ref, ck_ref, cv_ref, o_ref, bias_ref, *, lc, t):
    @pl.when(pl.program_id(0) == 0)
    def _():
        for h in range(N_HEADS):
            bias_ref[h] = _toeplitz_bias(r_ref[h], t)

    contract_last = (((1,), (1,)), ((), ()))
    for h in range(N_HEADS):
        cols = slice(h * HEAD_DIM, (h + 1) * HEAD_DIM)
        q = q_ref[:, cols].astype(BF16)
        kc = ck_ref[pl.ds(h, lc, stride=N_HEADS), :].astype(BF16)
        vc = cv_ref[pl.ds(h, lc, stride=N_HEADS), :].astype(BF16)
        kn = kn_ref[:, cols].astype(BF16)
        vn = vn_ref[:, cols].astype(BF16)
        s_c = lax.dot_general(q, kc, contract_last, preferred_element_type=F32) + bias_ref[h, :, :lc]
        s_n = lax.dot_general(q, kn, contract_last, preferred_element_type=F32) + bias_ref[h, :, lc:lc + t]
        m = jnp.maximum(jnp.max(s_c, axis=-1, keepdims=True), jnp.max(s_n, axis=-1, keepdims=True))
        p_c = jnp.exp(s_c - m)
        p_n = jnp.exp(s_n - m)
        l = jnp.sum(p_c, axis=-1, keepdims=True) + jnp.sum(p_n, axis=-1, keepdims=True)
        o = (jnp.dot(p_c.astype(BF16), vc, preferred_element_type=F32)
             + jnp.dot(p_n.astype(BF16), vn, preferred_element_type=F32))
        o_ref[:, cols] = (o / l).astype(BF16)


def _sample_attention(h, cache_k, cache_v, bias_rows, batch, t):
    lc = cache_k.shape[1] // N_HEADS
    assert lc == ATT_WINDOW and t <= CHUNK
    row = lambda seg: pl.BlockSpec((t, D_ATT), lambda b: (b, seg))
    cache = lambda: pl.BlockSpec((None, lc * N_HEADS, HEAD_DIM), lambda b: (b, 0, 0))
    return pl.pallas_call(
        functools.partial(_sample_attn_kernel, lc=lc, t=t),
        grid=(batch,),
        in_specs=[pl.BlockSpec((N_HEADS, 1, BIAS_PERIOD), lambda b: (0, 0, 0)),
                  row(0), row(1), row(2), cache(), cache()],
        out_specs=pl.BlockSpec((t, D_ATT), lambda b: (b, 0)),
        out_shape=jax.ShapeDtypeStruct((batch * t, D_ATT), BF16),
        scratch_shapes=[pltpu.VMEM((N_HEADS, t, BIAS_PERIOD), F32)],
        compiler_params=_params(("arbitrary",)),
        name="sample_attention",
    )(bias_rows, h, h, h, cache_k, cache_v)


HALO = POOL_HIST + 1
U_SEG = 3 * D_ATT // D_POOL


def _prompt_pool_kernel(u_ref, halo_ref, wp_ref, scale_ref, o_ref, full_ref, *, tm):
    t = pl.program_id(1)

    @pl.when(t == 0)
    def _():
        full_ref[0:HALO, :] = jnp.zeros((HALO, D_POOL), F32)

    @pl.when(t > 0)
    def _():
        full_ref[0:HALO, :] = halo_ref[...]

    full_ref[HALO:, :] = u_ref[...]
    pos = t * tm + lax.broadcasted_iota(jnp.int32, (tm, 1), 0)
    for g, w in enumerate(POOL_WINDOWS):
        cols = slice(g * POOL_GROUP_W, (g + 1) * POOL_GROUP_W)
        wsum = full_ref[HALO:, cols]
        for s in range(1, w):
            wsum = wsum + full_ref[HALO - s:HALO - s + tm, cols]
        cnt = jnp.minimum(w, pos + 1).astype(F32)
        d = wsum / cnt - full_ref[HALO:, cols]
        y = jnp.dot(d.astype(BF16), wp_ref[g], preferred_element_type=F32)
        o_ref[:, cols] = (y * scale_ref[:, cols]).astype(BF16)


def _prompt_pool(h, w_pool_b, pool_scale, batch, seq, tm):
    tiles = seq // tm
    halo_per_tile = tm // HALO
    return pl.pallas_call(
        functools.partial(_prompt_pool_kernel, tm=tm),
        grid=(batch, tiles),
        in_specs=[
            pl.BlockSpec((tm, D_POOL), lambda b, t: (b * tiles + t, U_SEG)),
            pl.BlockSpec((HALO, D_POOL),
                         lambda b, t: (jnp.maximum((b * tiles + t) * halo_per_tile - 1, 0), U_SEG)),
            pl.BlockSpec((len(POOL_WINDOWS), POOL_GROUP_W, POOL_GROUP_W), lambda b, t: (0, 0, 0)),
            pl.BlockSpec((1, D_POOL), lambda b, t: (0, 0)),
        ],
        out_specs=pl.BlockSpec((tm, D_POOL), lambda b, t: (b * tiles + t, 0)),
        out_shape=jax.ShapeDtypeStruct((batch * seq, D_POOL), BF16),
        scratch_shapes=[pltpu.VMEM((HALO + tm, D_POOL), F32)],
        compiler_params=_params(("arbitrary", "arbitrary")),
        name="prompt_pool",
    )(h, h, w_pool_b, pool_scale)


def _sample_pool_kernel(u_ref, state_ref, wp_ref, scale_ref, o_ref, new_state_ref, full_ref, *, batch, t):
    full_ref[:, 1:HALO, :] = state_ref[...]
    full_ref[:, HALO:, :] = u_ref[...].reshape(batch, t, D_POOL)
    for g, w in enumerate(POOL_WINDOWS):
        cols = slice(g * POOL_GROUP_W, (g + 1) * POOL_GROUP_W)
        wsum = full_ref[:, HALO:, cols]
        for s in range(1, w):
            wsum = wsum + full_ref[:, HALO - s:HALO - s + t, cols]
        d = wsum / float(w) - full_ref[:, HALO:, cols]
        y = jnp.dot(d.reshape(batch * t, POOL_GROUP_W).astype(BF16), wp_ref[g],
                    preferred_element_type=F32)
        o_ref[:, cols] = (y * scale_ref[:, cols]).astype(BF16)
    new_state_ref[...] = full_ref[:, HALO + t - POOL_HIST:, :]


def _sample_pool(h, state, w_pool_b, pool_scale, batch, t):
    full = lambda shape: pl.BlockSpec(shape, lambda i: (0,) * len(shape))
    return pl.pallas_call(
        functools.partial(_sample_pool_kernel, batch=batch, t=t),
        grid=(1,),
        in_specs=[pl.BlockSpec((batch * t, D_POOL), lambda i: (0, U_SEG)),
                  full((batch, POOL_HIST, D_POOL)),
                  full((len(POOL_WINDOWS), POOL_GROUP_W, POOL_GROUP_W)), full((1, D_POOL))],
        out_specs=[full((batch * t, D_POOL)), full((batch, POOL_HIST, D_POOL))],
        out_shape=[jax.ShapeDtypeStruct((batch * t, D_POOL), BF16),
                   jax.ShapeDtypeStruct((batch, POOL_HIST, D_POOL), F32)],
        scratch_shapes=[pltpu.VMEM((batch, HALO + t, D_POOL), F32)],
        compiler_params=_params(("arbitrary",)),
        name="sample_pool",
    )(h, state, w_pool_b, pool_scale)


LN_ROWS = 128


def _layer_norm_rows(h, g, b):
    mu = jnp.mean(h, axis=-1, keepdims=True)
    c = h - mu
    var = jnp.mean(c * c, axis=-1, keepdims=True)
    return c * lax.rsqrt(var + LN_EPS) * g + b


def _for_row_chunks(n_rows, body):
    assert n_rows % LN_ROWS == 0

    def step(c, carry):
        body(pl.ds(pl.multiple_of(c * LN_ROWS, LN_ROWS), LN_ROWS))
        return carry

    lax.fori_loop(0, n_rows // LN_ROWS, step, 0)


def _layer_norm_in_place(o_ref, g_ref, b_ref):
    def body(rows):
        o_ref[rows, :] = _layer_norm_rows(o_ref[rows, :], g_ref[...], b_ref[...])

    _for_row_chunks(o_ref.shape[0], body)


def _out_proj_kernel(att_ref, pool_ref, x_ref, w_ref, g_ref, b_ref, o_ref, *maybe_wb_ref, n_blocks, tn):
    j = pl.program_id(1)
    if maybe_wb_ref:
        wb_ref, = maybe_wb_ref
        wb_ref[...] = w_ref[...].astype(BF16)
    else:
        wb_ref = w_ref
    acc = jnp.dot(att_ref[...], wb_ref[:D_ATT, :], preferred_element_type=F32)
    acc = acc + jnp.dot(pool_ref[...], wb_ref[D_ATT:, :], preferred_element_type=F32)
    h = DEEPNORM_ALPHA * x_ref[...] + acc
    for n in range(n_blocks):
        @pl.when(j == n)
        def _():
            o_ref[:, n * tn:(n + 1) * tn] = h

    @pl.when(j == n_blocks - 1)
    def _():
        _layer_norm_in_place(o_ref, g_ref, b_ref)


def _out_proj_ln(att_o, pool_o, x, w, g, b, tm, tn):
    m = x.shape[0]
    n_blocks = D_MODEL // tn
    cast_weights = w.dtype == F32
    blocked = pl.BlockSpec((None, D_MODEL, tn), lambda i, j: (j, 0, 0))
    out_specs = [pl.BlockSpec((tm, D_MODEL), lambda i, j: (i, 0), pipeline_mode=pl.Buffered(1))]
    out_shape = [jax.ShapeDtypeStruct((m, D_MODEL), F32)]
    if cast_weights:
        assert m == tm
        w_spec = pl.BlockSpec((D_MODEL, tn), lambda i, j: (0, j))
        out_specs.append(blocked)
        out_shape.append(jax.ShapeDtypeStruct((n_blocks, D_MODEL, tn), BF16))
    else:
        w_spec = blocked
    return pl.pallas_call(
        functools.partial(_out_proj_kernel, n_blocks=n_blocks, tn=tn),
        grid=(m // tm, n_blocks),
        in_specs=[
            pl.BlockSpec((tm, D_ATT), lambda i, j: (i, 0)),
            pl.BlockSpec((tm, D_POOL), lambda i, j: (i, 0)),
            pl.BlockSpec((tm, tn), lambda i, j: (i, j)),
            w_spec,
            pl.BlockSpec((1, D_MODEL), lambda i, j: (0, 0)),
            pl.BlockSpec((1, D_MODEL), lambda i, j: (0, 0)),
        ],
        out_specs=out_specs,
        out_shape=out_shape,
        compiler_params=_params(("arbitrary", "arbitrary")),
        name="out_proj_ln",
    )(att_o, pool_o, x, w, g, b)


DOWN_COLS = 512


N_STAGE = 2


def _ffn_kernel(x_hbm, wg_ref, wu_ref, wd_ref, g_ref, b_ref, y_hbm, acc_ref, xb_ref, stage_ref,
                in_sem, out_sem, *, n_blocks, tm, n_tiles):
    i = pl.program_id(0)
    j = pl.program_id(1)
    n_chunks = tm // LN_ROWS

    def chunk_rows(c):
        return pl.ds(pl.multiple_of(c * LN_ROWS, LN_ROWS), LN_ROWS)

    def x_copy(tile, c):
        src = x_hbm.at[pl.ds(pl.multiple_of(tile * tm + c * LN_ROWS, LN_ROWS), LN_ROWS), :]
        return pltpu.make_async_copy(src, acc_ref.at[chunk_rows(c), :], in_sem.at[c])

    def y_copy(c, slot):
        dst = y_hbm.at[pl.ds(pl.multiple_of(i * tm + c * LN_ROWS, LN_ROWS), LN_ROWS), :]
        return pltpu.make_async_copy(stage_ref.at[slot], dst, out_sem.at[slot])

    @pl.when(j == 0)
    def _():
        @pl.when(i == 0)
        def _():
            for c in range(n_chunks):
                x_copy(0, c).start()

        def split(c, carry):
            x_copy(i, c).wait()
            x = acc_ref[chunk_rows(c), :]
            xb_ref[chunk_rows(c), :] = x.astype(BF16)
            acc_ref[chunk_rows(c), :] = DEEPNORM_ALPHA * x
            return carry

        lax.fori_loop(0, n_chunks, split, 0)

    xb = xb_ref[...]
    gate = jnp.dot(xb, wg_ref[...], preferred_element_type=F32)
    up = jnp.dot(xb, wu_ref[...], preferred_element_type=F32)
    act = (gate * (1.0 / (1.0 + jnp.exp(-gate))) * up).astype(BF16)
    for n in range(0, D_MODEL, DOWN_COLS):
        cols = slice(n, n + DOWN_COLS)
        acc_ref[:, cols] += jnp.dot(act, wd_ref[:, cols], preferred_element_type=F32)

    @pl.when(j == n_blocks - 1)
    def _():
        def finish(c2, carry):
            for slot in range(N_STAGE):
                c = c2 * N_STAGE + slot

                @pl.when(c2 > 0)
                def _():
                    y_copy(c - N_STAGE, slot).wait()

                stage_ref[slot] = _layer_norm_rows(acc_ref[chunk_rows(c), :], g_ref[...], b_ref[...])
                y_copy(c, slot).start()

                @pl.when(i + 1 < n_tiles)
                def _():
                    x_copy(i + 1, c).start()
            return carry

        lax.fori_loop(0, n_chunks // N_STAGE, finish, 0)
        for slot in range(N_STAGE):
            y_copy(n_chunks - N_STAGE + slot, slot).wait()


def _ffn_ln(x1, wg_blocks, wu_blocks, wd_b, g, b, tm):
    m = x1.shape[0]
    n_blocks, _, tf = wg_blocks.shape
    n_tiles = m // tm
    n_chunks = tm // LN_ROWS
    assert n_tiles * tm == m and n_chunks * LN_ROWS == tm and n_chunks % N_STAGE == 0
    return pl.pallas_call(
        functools.partial(_ffn_kernel, n_blocks=n_blocks, tm=tm, n_tiles=n_tiles),
        grid=(n_tiles, n_blocks),
        in_specs=[
            pl.BlockSpec(memory_space=pl.ANY),
            pl.BlockSpec((None, D_MODEL, tf), lambda i, j: (j, 0, 0)),
            pl.BlockSpec((None, D_MODEL, tf), lambda i, j: (j, 0, 0)),
            pl.BlockSpec((tf, D_MODEL), lambda i, j: (j, 0)),
            pl.BlockSpec((1, D_MODEL), lambda i, j: (0, 0)),
            pl.BlockSpec((1, D_MODEL), lambda i, j: (0, 0)),
        ],
        out_specs=pl.BlockSpec(memory_space=pl.ANY),
        out_shape=jax.ShapeDtypeStruct((m, D_MODEL), F32),
        scratch_shapes=[pltpu.VMEM((tm, D_MODEL), F32), pltpu.VMEM((tm, D_MODEL), BF16),
                        pltpu.VMEM((N_STAGE, LN_ROWS, D_MODEL), F32),
                        pltpu.SemaphoreType.DMA((n_chunks,)), pltpu.SemaphoreType.DMA((N_STAGE,))],
        compiler_params=_params(("arbitrary", "arbitrary")),
        name="ffn_ln",
    )(x1, wg_blocks, wu_blocks, wd_b, g, b)


IN_PROJ_COLS = 512
OUT_PROJ_COLS = 512
FFN_COLS = 256
ROW_TILE = 1024
W_DOWN_CAST_ROWS = 256


def kernel(x_prompt, x_sample, cache_k, cache_v, state_pool, w_in, rel_bias, w_pool, pool_scale,
           w_out, ln1_g, ln1_b, w_gate, w_up, w_down, ln2_g, ln2_b):
    batch, seq, _ = x_prompt.shape
    dec_batch, dec_seq, _ = x_sample.shape
    depth = w_in.shape[0]
    assert depth == 1 and seq % GROUP_ROWS == 0 and seq >= ATT_WINDOW
    lc = cache_k.shape[2]
    n_dec = dec_batch * dec_seq

    n_groups = len(POOL_WINDOWS)
    w_pool_b = _cast_rows(w_pool[0].reshape(n_groups * POOL_GROUP_W, POOL_GROUP_W),
                          n_groups * POOL_GROUP_W).reshape(n_groups, POOL_GROUP_W, POOL_GROUP_W)
    bias_rows = _bias_rows(rel_bias[0])
    g1, b1 = ln1_g[0][None], ln1_b[0][None]
    g2, b2 = ln2_g[0][None], ln2_b[0][None]
    scale = pool_scale[0][None]
    xp = x_prompt.reshape(batch * seq, D_MODEL)
    xs = x_sample.reshape(n_dec, D_MODEL)

    hs, w_in_b = _in_proj_cast(xs, w_in[0], IN_PROJ_COLS)
    hp, w_gate_b, w_up_b = _in_proj(xp, w_in_b, w_gate[0], w_up[0], ROW_TILE, FFN_COLS)

    att_s = _sample_attention(hs, cache_k[0].reshape(dec_batch, lc * N_HEADS, HEAD_DIM),
                              cache_v[0].reshape(dec_batch, lc * N_HEADS, HEAD_DIM),
                              bias_rows, dec_batch, dec_seq)
    pool_s, ps_new = _sample_pool(hs, state_pool[0], w_pool_b, scale, dec_batch, dec_seq)
    x1s, w_out_b = _out_proj_ln(att_s, pool_s, xs, w_out[0], g1, b1, n_dec, OUT_PROJ_COLS)

    att_p, w_down_b, kp_new, vp_new = _prompt_attention(hp, bias_rows, w_down[0], batch, seq,
                                                        W_DOWN_CAST_ROWS)
    pool_p = _prompt_pool(hp, w_pool_b, scale, batch, seq, 512)
    x1p, = _out_proj_ln(att_p, pool_p, xp, w_out_b, g1, b1, ROW_TILE, OUT_PROJ_COLS)

    ys = _ffn_ln(x1s, w_gate_b, w_up_b, w_down_b, g2, b2, n_dec)
    yp = _ffn_ln(x1p, w_gate_b, w_up_b, w_down_b, g2, b2, ROW_TILE)

    hp3 = hp.reshape(batch, seq, 4 * D_ATT)
    dec_shape = (1, dec_batch, dec_seq, N_HEADS, HEAD_DIM)
    return (yp.reshape(batch, seq, D_MODEL),
            ys.reshape(dec_batch, dec_seq, D_MODEL),
            kp_new,
            vp_new,
            hp3[None, :, seq - POOL_HIST:, 3 * D_ATT:],
            hs[:, D_ATT:2 * D_ATT].reshape(dec_shape),
            hs[:, 2 * D_ATT:3 * D_ATT].reshape(dec_shape),
            ps_new[None])
```

```python
import functools

import jax
import jax.numpy as jnp
from jax import lax
from jax.experimental import pallas as pl
from jax.experimental.pallas import tpu as pltpu

D_MODEL = 4096
CHUNK = 64
N_PREV_CHUNKS = 8
ATT_WINDOW = N_PREV_CHUNKS * CHUNK
D_ATT = D_MODEL // 2
D_POOL = D_MODEL - D_ATT
N_HEADS = 16
HEAD_DIM = D_ATT // N_HEADS
REL_CLIP = 256
POOL_WINDOWS = (2, 4, 8, 16)
POOL_GROUP_W = D_POOL // len(POOL_WINDOWS)
POOL_HIST = max(POOL_WINDOWS) - 1
LN_EPS = 1e-5
DEEPNORM_ALPHA = 2.0 ** 0.25
NEG_INF = -1e30
ATT_SCALE = HEAD_DIM ** -0.5
LOG2_E = 1.4426950408889634
Q_SCALE = ATT_SCALE * LOG2_E

GROUP_CHUNKS = 4
GROUP_ROWS = GROUP_CHUNKS * CHUNK
BAND_COLS = GROUP_ROWS + ATT_WINDOW
BIAS_PERIOD = 1024

VMEM_LIMIT_BYTES = 56 * 1024 * 1024

F32 = jnp.float32
BF16 = jnp.bfloat16


def _params(semantics):
    return pltpu.CompilerParams(dimension_semantics=semantics, vmem_limit_bytes=VMEM_LIMIT_BYTES)


def _cast_kernel(w_ref, o_ref):
    o_ref[...] = w_ref[...].astype(BF16)


def _cast_rows(w, rows):
    r, c = w.shape
    return pl.pallas_call(
        _cast_kernel,
        grid=(r // rows,),
        in_specs=[pl.BlockSpec((rows, c), lambda i: (i, 0))],
        out_specs=pl.BlockSpec((rows, c), lambda i: (i, 0)),
        out_shape=jax.ShapeDtypeStruct((r, c), BF16),
        compiler_params=_params(("arbitrary",)),
        name="cast_rows",
    )(w)


def _cast_to_col_blocks(w_ref, o_ref):
    nb, _, tf = o_ref.shape
    for c in range(nb):
        o_ref[c] = w_ref[:, c * tf:(c + 1) * tf].astype(BF16)


def _project(xb_ref, w, o_ref, j, q_blocks):
    acc = jnp.dot(xb_ref[...], w, preferred_element_type=F32)
    o_ref[...] = acc * jnp.where(j < q_blocks, Q_SCALE, 1.0)


def _in_proj_cast_kernel(x_ref, w_ref, o_ref, wb_ref, xb_ref, *, q_blocks):
    j = pl.program_id(0)

    @pl.when(j == 0)
    def _():
        xb_ref[...] = x_ref[...].astype(BF16)

    wb_ref[...] = w_ref[...].astype(BF16)
    _project(xb_ref, wb_ref[...], o_ref, j, q_blocks)


def _in_proj_cast(x, w, tn):
    m = x.shape[0]
    n = w.shape[1]
    return pl.pallas_call(
        functools.partial(_in_proj_cast_kernel, q_blocks=D_ATT // tn),
        grid=(n // tn,),
        in_specs=[
            pl.BlockSpec((m, D_MODEL), lambda j: (0, 0), pipeline_mode=pl.Buffered(1)),
            pl.BlockSpec((D_MODEL, tn), lambda j: (0, j)),
        ],
        out_specs=[pl.BlockSpec((m, tn), lambda j: (0, j)),
                   pl.BlockSpec((None, D_MODEL, tn), lambda j: (j, 0, 0))],
        out_shape=[jax.ShapeDtypeStruct((m, n), F32),
                   jax.ShapeDtypeStruct((n // tn, D_MODEL, tn), BF16)],
        scratch_shapes=[pltpu.VMEM((m, D_MODEL), BF16)],
        compiler_params=_params(("arbitrary",)),
        name="in_proj_cast",
    )(x, w)


def _in_proj_kernel(x_ref, w_ref, wg_ref, wu_ref, o_ref, wgb_ref, wub_ref, xb_ref, *, q_blocks):
    j = pl.program_id(1)

    @pl.when(j == 0)
    def _():
        xb_ref[...] = x_ref[...].astype(BF16)

    _project(xb_ref, w_ref[...], o_ref, j, q_blocks)
    _cast_to_col_blocks(wg_ref, wgb_ref)
    _cast_to_col_blocks(wu_ref, wub_ref)


def _in_proj(x, w_blocks, w_gate, w_up, tm, tf):
    m = x.shape[0]
    nb, _, tn = w_blocks.shape
    d_ff = w_gate.shape[1]
    steps = (m // tm) * nb
    slab = D_MODEL // steps
    assert slab * steps == D_MODEL and slab % 16 == 0 and d_ff % tf == 0
    side_in = lambda: pl.BlockSpec((slab, d_ff), lambda i, j: (i * nb + j, 0))
    side_out = lambda: pl.BlockSpec((d_ff // tf, slab, tf), lambda i, j: (0, i * nb + j, 0))
    side_shape = jax.ShapeDtypeStruct((d_ff // tf, D_MODEL, tf), BF16)
    return pl.pallas_call(
        functools.partial(_in_proj_kernel, q_blocks=D_ATT // tn),
        grid=(m // tm, nb),
        in_specs=[
            pl.BlockSpec((tm, D_MODEL), lambda i, j: (i, 0), pipeline_mode=pl.Buffered(1)),
            pl.BlockSpec((None, D_MODEL, tn), lambda i, j: (j, 0, 0)),
            side_in(), side_in(),
        ],
        out_specs=[pl.BlockSpec((tm, tn), lambda i, j: (i, j)), side_out(), side_out()],
        out_shape=[jax.ShapeDtypeStruct((m, nb * tn), F32), side_shape, side_shape],
        scratch_shapes=[pltpu.VMEM((tm, D_MODEL), BF16)],
        compiler_params=_params(("arbitrary", "arbitrary")),
        name="in_proj",
    )(x, w_blocks, w_gate, w_up)


def _bias_rows(table):
    far = table[:, 2 * REL_CLIP:]
    head = jnp.broadcast_to(far, (table.shape[0], ATT_WINDOW - REL_CLIP))
    tail = jnp.broadcast_to(far, (table.shape[0], BIAS_PERIOD - (ATT_WINDOW + REL_CLIP + 1)))
    return (LOG2_E * jnp.concatenate([head, table[:, ::-1], tail], axis=1))[:, None, :]


def _toeplitz_bias(r, rows):
    return pltpu.roll(jnp.broadcast_to(r, (rows, BIAS_PERIOD)), 0, 1, stride=1, stride_axis=0)


def _prompt_attn_kernel(r_ref, q_ref, k_ref, v_ref, wd_ref, o_ref, wdb_ref, kn_hbm, vn_hbm,
                        bias_ref, qb_ref, kb_ref, vb_ref, new_sem, *, seq, wd_blocks):
    keep = kn_hbm.shape[2]
    hd, b = pl.program_id(0), pl.program_id(1)
    new_copies = [
        pltpu.make_async_copy(src.at[pl.ds(seq - keep, keep), :], dst.at[0, b, :, hd, :], new_sem.at[n])
        for n, (src, dst) in enumerate(((k_ref, kn_hbm), (v_ref, vn_hbm)))]
    for cp in new_copies:
        cp.start()

    @pl.when(hd * pl.num_programs(1) + b < wd_blocks)
    def _():
        wdb_ref[...] = wd_ref[...].astype(BF16)

    @pl.when(b == 0)
    def _():
        t = _toeplitz_bias(r_ref[...], GROUP_ROWS)[:, :BAND_COLS]
        i = lax.broadcasted_iota(jnp.int32, (GROUP_ROWS, BAND_COLS), 0)
        j = lax.broadcasted_iota(jnp.int32, (GROUP_ROWS, BAND_COLS), 1)
        gap = j // CHUNK - i // CHUNK
        bias_ref[...] = jnp.where((gap >= 0) & (gap <= N_PREV_CHUNKS), t, NEG_INF)

    @pl.when((hd == 0) & (b == 0))
    def _():
        lane = lax.broadcasted_iota(jnp.int32, (seq, HEAD_DIM), 1)
        vb_ref[:, HEAD_DIM:] = jnp.where(lane == 0, 1.0, 0.0).astype(BF16)

    qb_ref[...] = q_ref[...].astype(BF16)
    kb_ref[...] = k_ref[...].astype(BF16)
    vb_ref[:, :HEAD_DIM] = v_ref[...].astype(BF16)
    for g in range(seq // GROUP_ROWS):
        q0 = g * GROUP_ROWS
        k0 = max(q0 - ATT_WINDOW, 0)
        nk = q0 + GROUP_ROWS - k0
        c0 = BAND_COLS - nk
        s = lax.dot_general(qb_ref[pl.ds(q0, GROUP_ROWS), :], kb_ref[pl.ds(k0, nk), :],
                            (((1,), (1,)), ((), ())), preferred_element_type=F32)
        s = s + bias_ref[:, c0:]
        m = jnp.max(s, axis=-1, keepdims=True)
        p = jnp.exp2(s - m).astype(BF16)
        o = jnp.dot(p, vb_ref[pl.ds(k0, nk), :], preferred_element_type=F32)
        o_ref[pl.ds(q0, GROUP_ROWS), :] = (o[:, :HEAD_DIM] / o[:, HEAD_DIM:HEAD_DIM + 1]).astype(BF16)
    for cp in new_copies:
        cp.wait()


def _prompt_attention(h, bias_rows, w_down, batch, seq, wd_rows):
    keep = min(ATT_WINDOW, seq)
    new_rows = lambda: pl.BlockSpec(memory_space=pl.ANY)
    new_shape = jax.ShapeDtypeStruct((1, batch, keep, N_HEADS, HEAD_DIM), F32)
    d_ff = w_down.shape[0]
    wd_blocks = d_ff // wd_rows
    assert wd_blocks * wd_rows == d_ff and wd_blocks <= N_HEADS * batch

    def col(seg):
        return pl.BlockSpec((seq, HEAD_DIM), lambda hd, b: (b, seg * N_HEADS + hd))

    wd_spec = lambda: pl.BlockSpec((wd_rows, D_MODEL),
                                   lambda hd, b: (jnp.minimum(hd * batch + b, wd_blocks - 1), 0))
    return pl.pallas_call(
        functools.partial(_prompt_attn_kernel, seq=seq, wd_blocks=wd_blocks),
        grid=(N_HEADS, batch),
        in_specs=[pl.BlockSpec((None, 1, BIAS_PERIOD), lambda hd, b: (hd, 0, 0)), col(0), col(1), col(2),
                  wd_spec()],
        out_specs=[pl.BlockSpec((seq, HEAD_DIM), lambda hd, b: (b, hd)), wd_spec(), new_rows(), new_rows()],
        out_shape=[jax.ShapeDtypeStruct((batch * seq, D_ATT), BF16),
                   jax.ShapeDtypeStruct((d_ff, D_MODEL), BF16), new_shape, new_shape],
        scratch_shapes=[pltpu.VMEM((GROUP_ROWS, BAND_COLS), F32)]
        + [pltpu.VMEM((seq, HEAD_DIM), BF16)] * 2 + [pltpu.VMEM((seq, 2 * HEAD_DIM), BF16)]
        + [pltpu.SemaphoreType.DMA((2,))],
        compiler_params=_params(("arbitrary", "arbitrary")),
        name="prompt_attention",
    )(bias_rows, h, h, h, w_down)


def _sample_attn_kernel(r_ref, q_ref, kn_ref, vn_ref, ck_ref, cv_ref, o_ref, bias_ref, *, lc, t):
    @pl.when(pl.program_id(0) == 0)
    def _():
        for h in range(N_HEADS):
            bias_ref[h] = _toeplitz_bias(r_ref[h], t)

    contract_last = (((1,), (1,)), ((), ()))
    for h in range(N_HEADS):
        cols = slice(h * HEAD_DIM, (h + 1) * HEAD_DIM)
        q = q_ref[:, cols].astype(BF16)
        kc = ck_ref[pl.ds(h, lc, stride=N_HEADS), :].astype(BF16)
        vc = cv_ref[pl.ds(h, lc, stride=N_HEADS), :].astype(BF16)
        kn = kn_ref[:, cols].astype(BF16)
        vn = vn_ref[:, cols].astype(BF16)
        s_c = lax.dot_general(q, kc, contract_last, preferred_element_type=F32) + bias_ref[h, :, :lc]
        s_n = lax.dot_general(q, kn, contract_last, preferred_element_type=F32) + bias_ref[h, :, lc:lc + t]
        m = jnp.maximum(jnp.max(s_c, axis=-1, keepdims=True), jnp.max(s_n, axis=-1, keepdims=True))
        p_c = jnp.exp2(s_c - m)
        p_n = jnp.exp2(s_n - m)
        l = jnp.sum(p_c, axis=-1, keepdims=True) + jnp.sum(p_n, axis=-1, keepdims=True)
        o = (jnp.dot(p_c.astype(BF16), vc, preferred_element_type=F32)
             + jnp.dot(p_n.astype(BF16), vn, preferred_element_type=F32))
        o_ref[:, cols] = (o / l).astype(BF16)


def _sample_attention(h, cache_k, cache_v, bias_rows, batch, t):
    lc = cache_k.shape[1] // N_HEADS
    assert lc == ATT_WINDOW and t <= CHUNK
    row = lambda seg: pl.BlockSpec((t, D_ATT), lambda b: (b, seg))
    cache = lambda: pl.BlockSpec((None, lc * N_HEADS, HEAD_DIM), lambda b: (b, 0, 0))
    return pl.pallas_call(
        functools.partial(_sample_attn_kernel, lc=lc, t=t),
        grid=(batch,),
        in_specs=[pl.BlockSpec((N_HEADS, 1, BIAS_PERIOD), lambda b: (0, 0, 0)),
                  row(0), row(1), row(2), cache(), cache()],
        out_specs=pl.BlockSpec((t, D_ATT), lambda b: (b, 0)),
        out_shape=jax.ShapeDtypeStruct((batch * t, D_ATT), BF16),
        scratch_shapes=[pltpu.VMEM((N_HEADS, t, BIAS_PERIOD), F32)],
        compiler_params=_params(("arbitrary",)),
        name="sample_attention",
    )(bias_rows, h, h, h, cache_k, cache_v)


HALO = POOL_HIST + 1
U_SEG = 3 * D_ATT // D_POOL
POOL_STAGES = max(POOL_WINDOWS).bit_length() - 1
POOL_HALO = 8 * POOL_STAGES


def _prompt_pool_kernel(u_ref, halo_ref, wp_ref, scale_ref, o_ref, full_ref, ping_ref, pong_ref, *, tm):
    t = pl.program_id(1)

    @pl.when(t == 0)
    def _():
        full_ref[0:POOL_HALO, :] = jnp.zeros((POOL_HALO, D_POOL), F32)

    @pl.when(t > 0)
    def _():
        full_ref[0:POOL_HALO, :] = halo_ref[...]

    full_ref[POOL_HALO:, :] = u_ref[...]
    rows = POOL_HALO + tm
    pos = t * tm + lax.broadcasted_iota(jnp.int32, (tm, 1), 0)
    for g, w in enumerate(POOL_WINDOWS):
        cols = slice(g * POOL_GROUP_W, (g + 1) * POOL_GROUP_W)
        src, src_cols = full_ref, cols
        for k in range(1, w.bit_length()):
            lo, back = 8 * k, 2 ** (k - 1)
            dst = (ping_ref, pong_ref)[k % 2]
            dst[lo:, :] = src[lo:, src_cols] + src[lo - back:rows - back, src_cols]
            src, src_cols = dst, slice(None)
        wsum = src[POOL_HALO:, src_cols]
        cnt = jnp.minimum(w, pos + 1).astype(F32)
        d = wsum / cnt - full_ref[POOL_HALO:, cols]
        y = jnp.dot(d.astype(BF16), wp_ref[g], preferred_element_type=F32)
        o_ref[:, cols] = (y * scale_ref[:, cols]).astype(BF16)


def _prompt_pool(h, w_pool_b, pool_scale, batch, seq, tm):
    tiles = seq // tm
    halo_per_tile = tm // POOL_HALO
    assert halo_per_tile * POOL_HALO == tm and POOL_HALO >= POOL_HIST
    stage = pltpu.VMEM((POOL_HALO + tm, POOL_GROUP_W), F32)
    return pl.pallas_call(
        functools.partial(_prompt_pool_kernel, tm=tm),
        grid=(batch, tiles),
        in_specs=[
            pl.BlockSpec((tm, D_POOL), lambda b, t: (b * tiles + t, U_SEG)),
            pl.BlockSpec((POOL_HALO, D_POOL),
                         lambda b, t: (jnp.maximum((b * tiles + t) * halo_per_tile - 1, 0), U_SEG)),
            pl.BlockSpec((len(POOL_WINDOWS), POOL_GROUP_W, POOL_GROUP_W), lambda b, t: (0, 0, 0)),
            pl.BlockSpec((1, D_POOL), lambda b, t: (0, 0)),
        ],
        out_specs=pl.BlockSpec((tm, D_POOL), lambda b, t: (b * tiles + t, 0)),
        out_shape=jax.ShapeDtypeStruct((batch * seq, D_POOL), BF16),
        scratch_shapes=[pltpu.VMEM((POOL_HALO + tm, D_POOL), F32), stage, stage],
        compiler_params=_params(("arbitrary", "arbitrary")),
        name="prompt_pool",
    )(h, h, w_pool_b, pool_scale)


def _sample_pool_kernel(u_ref, state_ref, wp_ref, scale_ref, o_ref, new_state_ref, full_ref, *, batch, t):
    full_ref[:, 1:HALO, :] = state_ref[...]
    full_ref[:, HALO:, :] = u_ref[...].reshape(batch, t, D_POOL)
    for g, w in enumerate(POOL_WINDOWS):
        cols = slice(g * POOL_GROUP_W, (g + 1) * POOL_GROUP_W)
        wsum = full_ref[:, HALO:, cols]
        for s in range(1, w):
            wsum = wsum + full_ref[:, HALO - s:HALO - s + t, cols]
        d = wsum / float(w) - full_ref[:, HALO:, cols]
        y = jnp.dot(d.reshape(batch * t, POOL_GROUP_W).astype(BF16), wp_ref[g],
                    preferred_element_type=F32)
        o_ref[:, cols] = (y * scale_ref[:, cols]).astype(BF16)
    new_state_ref[...] = full_ref[:, HALO + t - POOL_HIST:, :]


def _sample_pool(h, state, w_pool_b, pool_scale, batch, t):
    full = lambda shape: pl.BlockSpec(shape, lambda i: (0,) * len(shape))
    return pl.pallas_call(
        functools.partial(_sample_pool_kernel, batch=batch, t=t),
        grid=(1,),
        in_specs=[pl.BlockSpec((batch * t, D_POOL), lambda i: (0, U_SEG)),
                  full((batch, POOL_HIST, D_POOL)),
                  full((len(POOL_WINDOWS), POOL_GROUP_W, POOL_GROUP_W)), full((1, D_POOL))],
        out_specs=[full((batch * t, D_POOL)), full((batch, POOL_HIST, D_POOL))],
        out_shape=[jax.ShapeDtypeStruct((batch * t, D_POOL), BF16),
                   jax.ShapeDtypeStruct((batch, POOL_HIST, D_POOL), F32)],
        scratch_shapes=[pltpu.VMEM((batch, HALO + t, D_POOL), F32)],
        compiler_params=_params(("arbitrary",)),
        name="sample_pool",
    )(h, state, w_pool_b, pool_scale)


LN_ROWS = 128


def _layer_norm_rows(h, g, b):
    mu = jnp.mean(h, axis=-1, keepdims=True)
    c = h - mu
    var = jnp.mean(c * c, axis=-1, keepdims=True)
    return c * lax.rsqrt(var + LN_EPS) * g + b


N_STAGE = 2


def _chunk_rows(c, base=0):
    return pl.ds(pl.multiple_of(base + c * LN_ROWS, LN_ROWS), LN_ROWS)


def _layer_norm_to_hbm(src_ref, g_ref, b_ref, dst_hbm, row0, stage_ref, sem, after_chunk=None):
    n_chunks = src_ref.shape[0] // LN_ROWS
    assert n_chunks * LN_ROWS == src_ref.shape[0] and n_chunks % N_STAGE == 0

    def copy(c, slot):
        return pltpu.make_async_copy(stage_ref.at[slot], dst_hbm.at[_chunk_rows(c, row0), :], sem.at[slot])

    def body(c2, carry):
        for slot in range(N_STAGE):
            c = c2 * N_STAGE + slot

            @pl.when(c2 > 0)
            def _():
                copy(c - N_STAGE, slot).wait()

            stage_ref[slot] = _layer_norm_rows(src_ref[_chunk_rows(c), :], g_ref[...], b_ref[...])
            copy(c, slot).start()
            if after_chunk is not None:
                after_chunk(c)
        return carry

    lax.fori_loop(0, n_chunks // N_STAGE, body, 0)
    for slot in range(N_STAGE):
        copy(n_chunks - N_STAGE + slot, slot).wait()


def _stage_scratch():
    return [pltpu.VMEM((N_STAGE, LN_ROWS, D_MODEL), F32), pltpu.SemaphoreType.DMA((N_STAGE,))]


def _out_proj_kernel(att_ref, pool_ref, x_ref, w_ref, g_ref, b_ref, o_hbm, *rest, n_blocks, tn):
    *maybe_wb_ref, h_ref, stage_ref, out_sem = rest
    i = pl.program_id(0)
    j = pl.program_id(1)
    if maybe_wb_ref:
        wb_ref, = maybe_wb_ref
        wb_ref[...] = w_ref[...].astype(BF16)
    else:
        wb_ref = w_ref
    acc = jnp.dot(att_ref[...], wb_ref[:D_ATT, :], preferred_element_type=F32)
    acc = acc + jnp.dot(pool_ref[...], wb_ref[D_ATT:, :], preferred_element_type=F32)
    h = DEEPNORM_ALPHA * x_ref[...] + acc
    for n in range(n_blocks):
        @pl.when(j == n)
        def _():
            h_ref[:, n * tn:(n + 1) * tn] = h

    @pl.when(j == n_blocks - 1)
    def _():
        _layer_norm_to_hbm(h_ref, g_ref, b_ref, o_hbm, i * h_ref.shape[0], stage_ref, out_sem)


def _out_proj_ln(att_o, pool_o, x, w, g, b, tm, tn):
    m = x.shape[0]
    n_blocks = D_MODEL // tn
    cast_weights = w.dtype == F32
    blocked = pl.BlockSpec((None, D_MODEL, tn), lambda i, j: (j, 0, 0))
    out_specs = [pl.BlockSpec(memory_space=pl.ANY)]
    out_shape = [jax.ShapeDtypeStruct((m, D_MODEL), F32)]
    if cast_weights:
        assert m == tm
        w_spec = pl.BlockSpec((D_MODEL, tn), lambda i, j: (0, j))
        out_specs.append(blocked)
        out_shape.append(jax.ShapeDtypeStruct((n_blocks, D_MODEL, tn), BF16))
    else:
        w_spec = blocked
    return pl.pallas_call(
        functools.partial(_out_proj_kernel, n_blocks=n_blocks, tn=tn),
        grid=(m // tm, n_blocks),
        in_specs=[
            pl.BlockSpec((tm, D_ATT), lambda i, j: (i, 0)),
            pl.BlockSpec((tm, D_POOL), lambda i, j: (i, 0)),
            pl.BlockSpec((tm, tn), lambda i, j: (i, j)),
            w_spec,
            pl.BlockSpec((1, D_MODEL), lambda i, j: (0, 0)),
            pl.BlockSpec((1, D_MODEL), lambda i, j: (0, 0)),
        ],
        out_specs=out_specs,
        out_shape=out_shape,
        scratch_shapes=[pltpu.VMEM((tm, D_MODEL), F32)] + _stage_scratch(),
        compiler_params=_params(("arbitrary", "arbitrary")),
        name="out_proj_ln",
    )(att_o, pool_o, x, w, g, b)


DOWN_COLS = 512


def _ffn_kernel(x_hbm, wg_ref, wu_ref, wd_ref, g_ref, b_ref, y_hbm, acc_ref, xb_ref, in_sem, stage_ref,
                out_sem, *, n_blocks, tm, n_tiles):
    i = pl.program_id(0)
    j = pl.program_id(1)
    n_chunks = tm // LN_ROWS

    def x_copy(tile, c):
        return pltpu.make_async_copy(x_hbm.at[_chunk_rows(c, tile * tm), :], acc_ref.at[_chunk_rows(c), :],
                                     in_sem.at[c])

    @pl.when(j == 0)
    def _():
        @pl.when(i == 0)
        def _():
            for c in range(n_chunks):
                x_copy(0, c).start()

        def split(c, carry):
            x_copy(i, c).wait()
            x = acc_ref[_chunk_rows(c), :]
            xb_ref[_chunk_rows(c), :] = x.astype(BF16)
            acc_ref[_chunk_rows(c), :] = DEEPNORM_ALPHA * x
            return carry

        lax.fori_loop(0, n_chunks, split, 0)

    xb = xb_ref[...]
    gate = jnp.dot(xb, wg_ref[...], preferred_element_type=F32)
    up = jnp.dot(xb, wu_ref[...], preferred_element_type=F32)
    act = (gate * (1.0 / (1.0 + jnp.exp(-gate))) * up).astype(BF16)
    for n in range(0, D_MODEL, DOWN_COLS):
        cols = slice(n, n + DOWN_COLS)
        acc_ref[:, cols] += jnp.dot(act, wd_ref[:, cols], preferred_element_type=F32)

    @pl.when(j == n_blocks - 1)
    def _():
        def prefetch(c):
            @pl.when(i + 1 < n_tiles)
            def _():
                x_copy(i + 1, c).start()

        _layer_norm_to_hbm(acc_ref, g_ref, b_ref, y_hbm, i * tm, stage_ref, out_sem, after_chunk=prefetch)


def _ffn_ln(x1, wg_blocks, wu_blocks, wd_b, g, b, tm):
    m = x1.shape[0]
    n_blocks, _, tf = wg_blocks.shape
    n_tiles = m // tm
    n_chunks = tm // LN_ROWS
    assert n_tiles * tm == m and n_chunks * LN_ROWS == tm
    return pl.pallas_call(
        functools.partial(_ffn_kernel, n_blocks=n_blocks, tm=tm, n_tiles=n_tiles),
        grid=(n_tiles, n_blocks),
        in_specs=[
            pl.BlockSpec(memory_space=pl.ANY),
            pl.BlockSpec((None, D_MODEL, tf), lambda i, j: (j, 0, 0)),
            pl.BlockSpec((None, D_MODEL, tf), lambda i, j: (j, 0, 0)),
            pl.BlockSpec((tf, D_MODEL), lambda i, j: (j, 0)),
            pl.BlockSpec((1, D_MODEL), lambda i, j: (0, 0)),
            pl.BlockSpec((1, D_MODEL), lambda i, j: (0, 0)),
        ],
        out_specs=pl.BlockSpec(memory_space=pl.ANY),
        out_shape=jax.ShapeDtypeStruct((m, D_MODEL), F32),
        scratch_shapes=[pltpu.VMEM((tm, D_MODEL), F32), pltpu.VMEM((tm, D_MODEL), BF16),
                        pltpu.SemaphoreType.DMA((n_chunks,))] + _stage_scratch(),
        compiler_params=_params(("arbitrary", "arbitrary")),
        name="ffn_ln",
    )(x1, wg_blocks, wu_blocks, wd_b, g, b)


IN_PROJ_COLS = 512
OUT_PROJ_COLS = 512
FFN_COLS = 256
ROW_TILE = 1024
W_DOWN_CAST_ROWS = 256
POOL_ROWS = 512


def kernel(x_prompt, x_sample, cache_k, cache_v, state_pool, w_in, rel_bias, w_pool, pool_scale,
           w_out, ln1_g, ln1_b, w_gate, w_up, w_down, ln2_g, ln2_b):
    batch, seq, _ = x_prompt.shape
    dec_batch, dec_seq, _ = x_sample.shape
    depth = w_in.shape[0]
    assert depth == 1 and seq % GROUP_ROWS == 0 and seq >= ATT_WINDOW
    lc = cache_k.shape[2]
    n_dec = dec_batch * dec_seq

    n_groups = len(POOL_WINDOWS)
    w_pool_b = _cast_rows(w_pool[0].reshape(n_groups * POOL_GROUP_W, POOL_GROUP_W),
                          n_groups * POOL_GROUP_W).reshape(n_groups, POOL_GROUP_W, POOL_GROUP_W)
    bias_rows = _bias_rows(rel_bias[0])
    g1, b1 = ln1_g[0][None], ln1_b[0][None]
    g2, b2 = ln2_g[0][None], ln2_b[0][None]
    scale = pool_scale[0][None]
    xp = x_prompt.reshape(batch * seq, D_MODEL)
    xs = x_sample.reshape(n_dec, D_MODEL)

    hs, w_in_b = _in_proj_cast(xs, w_in[0], IN_PROJ_COLS)
    hp, w_gate_b, w_up_b = _in_proj(xp, w_in_b, w_gate[0], w_up[0], ROW_TILE, FFN_COLS)

    att_s = _sample_attention(hs, cache_k[0].reshape(dec_batch, lc * N_HEADS, HEAD_DIM),
                              cache_v[0].reshape(dec_batch, lc * N_HEADS, HEAD_DIM),
                              bias_rows, dec_batch, dec_seq)
    pool_s, ps_new = _sample_pool(hs, state_pool[0], w_pool_b, scale, dec_batch, dec_seq)
    x1s, w_out_b = _out_proj_ln(att_s, pool_s, xs, w_out[0], g1, b1, n_dec, OUT_PROJ_COLS)

    att_p, w_down_b, kp_new, vp_new = _prompt_attention(hp, bias_rows, w_down[0], batch, seq,
                                                        W_DOWN_CAST_ROWS)
    pool_p = _prompt_pool(hp, w_pool_b, scale, batch, seq, POOL_ROWS)
    x1p, = _out_proj_ln(att_p, pool_p, xp, w_out_b, g1, b1, ROW_TILE, OUT_PROJ_COLS)

    ys = _ffn_ln(x1s, w_gate_b, w_up_b, w_down_b, g2, b2, n_dec)
    yp = _ffn_ln(x1p, w_gate_b, w_up_b, w_down_b, g2, b2, ROW_TILE)

    hp3 = hp.reshape(batch, seq, 4 * D_ATT)
    dec_shape = (1, dec_batch, dec_seq, N_HEADS, HEAD_DIM)
    return (yp.reshape(batch, seq, D_MODEL),
            ys.reshape(dec_batch, dec_seq, D_MODEL),
            kp_new,
            vp_new,
            hp3[None, :, seq - POOL_HIST:, 3 * D_ATT:],
            hs[:, D_ATT:2 * D_ATT].reshape(dec_shape),
            hs[:, 2 * D_ATT:3 * D_ATT].reshape(dec_shape),
            ps_new[None])
```

```python
import functools

import jax
import jax.numpy as jnp
from jax import lax
from jax.experimental import pallas as pl
from jax.experimental.pallas import tpu as pltpu

D_MODEL = 4096
CHUNK = 64
N_PREV_CHUNKS = 8
ATT_WINDOW = N_PREV_CHUNKS * CHUNK
D_ATT = D_MODEL // 2
D_POOL = D_MODEL - D_ATT
N_HEADS = 16
HEAD_DIM = D_ATT // N_HEADS
REL_CLIP = 256
POOL_WINDOWS = (2, 4, 8, 16)
POOL_GROUP_W = D_POOL // len(POOL_WINDOWS)
POOL_HIST = max(POOL_WINDOWS) - 1
LN_EPS = 1e-5
DEEPNORM_ALPHA = 2.0 ** 0.25
NEG_INF = -1e30
ATT_SCALE = HEAD_DIM ** -0.5
LOG2_E = 1.4426950408889634
Q_SCALE = ATT_SCALE * LOG2_E

GROUP_CHUNKS = 4
GROUP_ROWS = GROUP_CHUNKS * CHUNK
BAND_COLS = GROUP_ROWS + ATT_WINDOW
BIAS_PERIOD = 1024

VMEM_LIMIT_BYTES = 56 * 1024 * 1024

F32 = jnp.float32
BF16 = jnp.bfloat16


def _params(semantics):
    return pltpu.CompilerParams(dimension_semantics=semantics, vmem_limit_bytes=VMEM_LIMIT_BYTES)


def _cast_kernel(w_ref, o_ref):
    o_ref[...] = w_ref[...].astype(BF16)


def _cast_rows(w, rows):
    r, c = w.shape
    return pl.pallas_call(
        _cast_kernel,
        grid=(r // rows,),
        in_specs=[pl.BlockSpec((rows, c), lambda i: (i, 0))],
        out_specs=pl.BlockSpec((rows, c), lambda i: (i, 0)),
        out_shape=jax.ShapeDtypeStruct((r, c), BF16),
        compiler_params=_params(("arbitrary",)),
        name="cast_rows",
    )(w)


def _cast_to_col_blocks(w_ref, o_ref):
    nb, _, tf = o_ref.shape
    for c in range(nb):
        o_ref[c] = w_ref[:, c * tf:(c + 1) * tf].astype(BF16)


def _project(xb_ref, w, o_ref, j, q_blocks):
    acc = jnp.dot(xb_ref[...], w, preferred_element_type=F32)
    o_ref[...] = acc * jnp.where(j < q_blocks, Q_SCALE, 1.0)


def _in_proj_cast_kernel(x_ref, w_ref, o_ref, wb_ref, xb_ref, *, q_blocks):
    j = pl.program_id(0)

    @pl.when(j == 0)
    def _():
        xb_ref[...] = x_ref[...].astype(BF16)

    wb_ref[...] = w_ref[...].astype(BF16)
    _project(xb_ref, wb_ref[...], o_ref, j, q_blocks)


def _in_proj_cast(x, w, tn):
    m = x.shape[0]
    n = w.shape[1]
    return pl.pallas_call(
        functools.partial(_in_proj_cast_kernel, q_blocks=D_ATT // tn),
        grid=(n // tn,),
        in_specs=[
            pl.BlockSpec((m, D_MODEL), lambda j: (0, 0), pipeline_mode=pl.Buffered(1)),
            pl.BlockSpec((D_MODEL, tn), lambda j: (0, j)),
        ],
        out_specs=[pl.BlockSpec((m, tn), lambda j: (0, j)),
                   pl.BlockSpec((None, D_MODEL, tn), lambda j: (j, 0, 0))],
        out_shape=[jax.ShapeDtypeStruct((m, n), F32),
                   jax.ShapeDtypeStruct((n // tn, D_MODEL, tn), BF16)],
        scratch_shapes=[pltpu.VMEM((m, D_MODEL), BF16)],
        compiler_params=_params(("arbitrary",)),
        name="in_proj_cast",
    )(x, w)


X_CHUNK = 128
X_SLOTS = 2


def _in_proj_kernel(x_hbm, w_ref, wg_ref, wu_ref, o_ref, wgb_ref, wub_ref, xb_ref, xs_ref, x_sem,
                    *, q_blocks, tm, n_tiles):
    i = pl.program_id(0)
    j = pl.program_id(1)
    n_chunks = tm // X_CHUNK

    def fetch(tile, c):
        src = x_hbm.at[pl.ds(pl.multiple_of(tile * tm + c * X_CHUNK, X_CHUNK), X_CHUNK), :]
        return pltpu.make_async_copy(src, xs_ref.at[c % X_SLOTS], x_sem.at[c % X_SLOTS])

    def convert(tile, c):
        fetch(tile, c).wait()
        rows = pl.ds(pl.multiple_of(c * X_CHUNK, X_CHUNK), X_CHUNK)
        xb_ref[tile % 2, rows, :] = xs_ref[c % X_SLOTS].astype(BF16)

    @pl.when((i == 0) & (j == 0))
    def _():
        for c in range(X_SLOTS):
            fetch(0, c).start()

        def first_tile(c, carry):
            convert(0, c)

            @pl.when(c + X_SLOTS < n_chunks)
            def _():
                fetch(0, c + X_SLOTS).start()
            return carry

        lax.fori_loop(0, n_chunks, first_tile, 0)

    @pl.when(i + 1 < n_tiles)
    def _():
        @pl.when((j >= 1) & (j <= n_chunks))
        def _():
            convert(i + 1, j - 1)

        @pl.when(j < n_chunks)
        def _():
            fetch(i + 1, j).start()

    _project(xb_ref.at[i % 2], w_ref[...], o_ref, j, q_blocks)
    _cast_to_col_blocks(wg_ref, wgb_ref)
    _cast_to_col_blocks(wu_ref, wub_ref)


def _in_proj(x, w_blocks, w_gate, w_up, tm, tf):
    m = x.shape[0]
    nb, _, tn = w_blocks.shape
    d_ff = w_gate.shape[1]
    n_tiles = m // tm
    steps = n_tiles * nb
    slab = D_MODEL // steps
    assert slab * steps == D_MODEL and slab % 16 == 0 and d_ff % tf == 0
    n_chunks = tm // X_CHUNK
    assert n_tiles * tm == m and n_chunks * X_CHUNK == tm and X_SLOTS <= n_chunks < nb
    side_in = lambda: pl.BlockSpec((slab, d_ff), lambda i, j: (i * nb + j, 0))
    side_out = lambda: pl.BlockSpec((d_ff // tf, slab, tf), lambda i, j: (0, i * nb + j, 0))
    side_shape = jax.ShapeDtypeStruct((d_ff // tf, D_MODEL, tf), BF16)
    return pl.pallas_call(
        functools.partial(_in_proj_kernel, q_blocks=D_ATT // tn, tm=tm, n_tiles=n_tiles),
        grid=(n_tiles, nb),
        in_specs=[
            pl.BlockSpec(memory_space=pl.ANY),
            pl.BlockSpec((None, D_MODEL, tn), lambda i, j: (j, 0, 0)),
            side_in(), side_in(),
        ],
        out_specs=[pl.BlockSpec((tm, tn), lambda i, j: (i, j)), side_out(), side_out()],
        out_shape=[jax.ShapeDtypeStruct((m, nb * tn), F32), side_shape, side_shape],
        scratch_shapes=[pltpu.VMEM((2, tm, D_MODEL), BF16), pltpu.VMEM((X_SLOTS, X_CHUNK, D_MODEL), F32),
                        pltpu.SemaphoreType.DMA((X_SLOTS,))],
        compiler_params=_params(("arbitrary", "arbitrary")),
        name="in_proj",
    )(x, w_blocks, w_gate, w_up)


def _bias_rows(table):
    far = table[:, 2 * REL_CLIP:]
    head = jnp.broadcast_to(far, (table.shape[0], ATT_WINDOW - REL_CLIP))
    tail = jnp.broadcast_to(far, (table.shape[0], BIAS_PERIOD - (ATT_WINDOW + REL_CLIP + 1)))
    return (LOG2_E * jnp.concatenate([head, table[:, ::-1], tail], axis=1))[:, None, :]


def _toeplitz_bias(r, rows):
    return pltpu.roll(jnp.broadcast_to(r, (rows, BIAS_PERIOD)), 0, 1, stride=1, stride_axis=0)


def _prompt_attn_kernel(r_ref, q_ref, k_ref, v_ref, wd_ref, o_ref, wdb_ref, kn_hbm, vn_hbm,
                        bias_ref, qb_ref, kb_ref, vb_ref, new_sem, *, seq, wd_blocks):
    keep = kn_hbm.shape[2]
    hd, b = pl.program_id(0), pl.program_id(1)
    new_copies = [
        pltpu.make_async_copy(src.at[pl.ds(seq - keep, keep), :], dst.at[0, b, :, hd, :], new_sem.at[n])
        for n, (src, dst) in enumerate(((k_ref, kn_hbm), (v_ref, vn_hbm)))]
    for cp in new_copies:
        cp.start()

    @pl.when(hd * pl.num_programs(1) + b < wd_blocks)
    def _():
        wdb_ref[...] = wd_ref[...].astype(BF16)

    @pl.when(b == 0)
    def _():
        t = _toeplitz_bias(r_ref[...], GROUP_ROWS)[:, :BAND_COLS]
        i = lax.broadcasted_iota(jnp.int32, (GROUP_ROWS, BAND_COLS), 0)
        j = lax.broadcasted_iota(jnp.int32, (GROUP_ROWS, BAND_COLS), 1)
        gap = j // CHUNK - i // CHUNK
        bias_ref[...] = jnp.where((gap >= 0) & (gap <= N_PREV_CHUNKS), t, NEG_INF)

    @pl.when((hd == 0) & (b == 0))
    def _():
        lane = lax.broadcasted_iota(jnp.int32, (seq, HEAD_DIM), 1)
        vb_ref[:, HEAD_DIM:] = jnp.where(lane == 0, 1.0, 0.0).astype(BF16)

    qb_ref[...] = q_ref[...].astype(BF16)
    kb_ref[...] = k_ref[...].astype(BF16)
    vb_ref[:, :HEAD_DIM] = v_ref[...].astype(BF16)
    for g in range(seq // GROUP_ROWS):
        q0 = g * GROUP_ROWS
        k0 = max(q0 - ATT_WINDOW, 0)
        nk = q0 + GROUP_ROWS - k0
        c0 = BAND_COLS - nk
        s = lax.dot_general(qb_ref[pl.ds(q0, GROUP_ROWS), :], kb_ref[pl.ds(k0, nk), :],
                            (((1,), (1,)), ((), ())), preferred_element_type=F32)
        s = s + bias_ref[:, c0:]
        m = jnp.max(s, axis=-1, keepdims=True)
        p = jnp.exp2(s - m).astype(BF16)
        o = jnp.dot(p, vb_ref[pl.ds(k0, nk), :], preferred_element_type=F32)
        o_ref[pl.ds(q0, GROUP_ROWS), :] = (o[:, :HEAD_DIM] / o[:, HEAD_DIM:HEAD_DIM + 1]).astype(BF16)
    for cp in new_copies:
        cp.wait()


def _prompt_attention(h, bias_rows, w_down, batch, seq, wd_rows):
    keep = min(ATT_WINDOW, seq)
    new_rows = lambda: pl.BlockSpec(memory_space=pl.ANY)
    new_shape = jax.ShapeDtypeStruct((1, batch, keep, N_HEADS, HEAD_DIM), F32)
    d_ff = w_down.shape[0]
    wd_blocks = d_ff // wd_rows
    assert wd_blocks * wd_rows == d_ff and wd_blocks <= N_HEADS * batch

    def col(seg):
        return pl.BlockSpec((seq, HEAD_DIM), lambda hd, b: (b, seg * N_HEADS + hd))

    wd_spec = lambda: pl.BlockSpec((wd_rows, D_MODEL),
                                   lambda hd, b: (jnp.minimum(hd * batch + b, wd_blocks - 1), 0))
    return pl.pallas_call(
        functools.partial(_prompt_attn_kernel, seq=seq, wd_blocks=wd_blocks),
        grid=(N_HEADS, batch),
        in_specs=[pl.BlockSpec((None, 1, BIAS_PERIOD), lambda hd, b: (hd, 0, 0)), col(0), col(1), col(2),
                  wd_spec()],
        out_specs=[pl.BlockSpec((seq, HEAD_DIM), lambda hd, b: (b, hd)), wd_spec(), new_rows(), new_rows()],
        out_shape=[jax.ShapeDtypeStruct((batch * seq, D_ATT), BF16),
                   jax.ShapeDtypeStruct((d_ff, D_MODEL), BF16), new_shape, new_shape],
        scratch_shapes=[pltpu.VMEM((GROUP_ROWS, BAND_COLS), F32)]
        + [pltpu.VMEM((seq, HEAD_DIM), BF16)] * 2 + [pltpu.VMEM((seq, 2 * HEAD_DIM), BF16)]
        + [pltpu.SemaphoreType.DMA((2,))],
        compiler_params=_params(("arbitrary", "arbitrary")),
        name="prompt_attention",
    )(bias_rows, h, h, h, w_down)


def _sample_attn_kernel(r_ref, q_ref, kn_ref, vn_ref, ck_ref, cv_ref, o_ref, bias_ref, *, lc, t):
    @pl.when(pl.program_id(0) == 0)
    def _():
        for h in range(N_HEADS):
            bias_ref[h] = _toeplitz_bias(r_ref[h], t)

    contract_last = (((1,), (1,)), ((), ()))
    for h in range(N_HEADS):
        cols = slice(h * HEAD_DIM, (h + 1) * HEAD_DIM)
        q = q_ref[:, cols].astype(BF16)
        kc = ck_ref[pl.ds(h, lc, stride=N_HEADS), :].astype(BF16)
        vc = cv_ref[pl.ds(h, lc, stride=N_HEADS), :].astype(BF16)
        kn = kn_ref[:, cols].astype(BF16)
        vn = vn_ref[:, cols].astype(BF16)
        s_c = lax.dot_general(q, kc, contract_last, preferred_element_type=F32) + bias_ref[h, :, :lc]
        s_n = lax.dot_general(q, kn, contract_last, preferred_element_type=F32) + bias_ref[h, :, lc:lc + t]
        m = jnp.maximum(jnp.max(s_c, axis=-1, keepdims=True), jnp.max(s_n, axis=-1, keepdims=True))
        p_c = jnp.exp2(s_c - m)
        p_n = jnp.exp2(s_n - m)
        l = jnp.sum(p_c, axis=-1, keepdims=True) + jnp.sum(p_n, axis=-1, keepdims=True)
        o = (jnp.dot(p_c.astype(BF16), vc, preferred_element_type=F32)
             + jnp.dot(p_n.astype(BF16), vn, preferred_element_type=F32))
        o_ref[:, cols] = (o / l).astype(BF16)


def _sample_attention(h, cache_k, cache_v, bias_rows, batch, t):
    lc = cache_k.shape[1] // N_HEADS
    assert lc == ATT_WINDOW and t <= CHUNK
    row = lambda seg: pl.BlockSpec((t, D_ATT), lambda b: (b, seg))
    cache = lambda: pl.BlockSpec((None, lc * N_HEADS, HEAD_DIM), lambda b: (b, 0, 0))
    return pl.pallas_call(
        functools.partial(_sample_attn_kernel, lc=lc, t=t),
        grid=(batch,),
        in_specs=[pl.BlockSpec((N_HEADS, 1, BIAS_PERIOD), lambda b: (0, 0, 0)),
                  row(0), row(1), row(2), cache(), cache()],
        out_specs=pl.BlockSpec((t, D_ATT), lambda b: (b, 0)),
        out_shape=jax.ShapeDtypeStruct((batch * t, D_ATT), BF16),
        scratch_shapes=[pltpu.VMEM((N_HEADS, t, BIAS_PERIOD), F32)],
        compiler_params=_params(("arbitrary",)),
        name="sample_attention",
    )(bias_rows, h, h, h, cache_k, cache_v)


HALO = POOL_HIST + 1
U_SEG = 3 * D_ATT // D_POOL
POOL_STAGES = max(POOL_WINDOWS).bit_length() - 1
POOL_HALO = 8 * POOL_STAGES


def _prompt_pool_kernel(u_ref, halo_ref, wp_ref, scale_ref, o_ref, full_ref, ping_ref, pong_ref, *, tm):
    t = pl.program_id(1)

    @pl.when(t == 0)
    def _():
        full_ref[0:POOL_HALO, :] = jnp.zeros((POOL_HALO, D_POOL), F32)

    @pl.when(t > 0)
    def _():
        full_ref[0:POOL_HALO, :] = halo_ref[...]

    full_ref[POOL_HALO:, :] = u_ref[...]
    rows = POOL_HALO + tm
    pos = t * tm + lax.broadcasted_iota(jnp.int32, (tm, 1), 0)
    for g, w in enumerate(POOL_WINDOWS):
        cols = slice(g * POOL_GROUP_W, (g + 1) * POOL_GROUP_W)
        src, src_cols = full_ref, cols
        for k in range(1, w.bit_length()):
            lo, back = 8 * k, 2 ** (k - 1)
            dst = (ping_ref, pong_ref)[k % 2]
            dst[lo:, :] = src[lo:, src_cols] + src[lo - back:rows - back, src_cols]
            src, src_cols = dst, slice(None)
        wsum = src[POOL_HALO:, src_cols]
        cnt = jnp.minimum(w, pos + 1).astype(F32)
        d = wsum / cnt - full_ref[POOL_HALO:, cols]
        y = jnp.dot(d.astype(BF16), wp_ref[g], preferred_element_type=F32)
        o_ref[:, cols] = (y * scale_ref[:, cols]).astype(BF16)


def _prompt_pool(h, w_pool_b, pool_scale, batch, seq, tm):
    tiles = seq // tm
    halo_per_tile = tm // POOL_HALO
    assert halo_per_tile * POOL_HALO == tm and POOL_HALO >= POOL_HIST
    stage = pltpu.VMEM((POOL_HALO + tm, POOL_GROUP_W), F32)
    return pl.pallas_call(
        functools.partial(_prompt_pool_kernel, tm=tm),
        grid=(batch, tiles),
        in_specs=[
            pl.BlockSpec((tm, D_POOL), lambda b, t: (b * tiles + t, U_SEG)),
            pl.BlockSpec((POOL_HALO, D_POOL),
                         lambda b, t: (jnp.maximum((b * tiles + t) * halo_per_tile - 1, 0), U_SEG)),
            pl.BlockSpec((len(POOL_WINDOWS), POOL_GROUP_W, POOL_GROUP_W), lambda b, t: (0, 0, 0)),
            pl.BlockSpec((1, D_POOL), lambda b, t: (0, 0)),
        ],
        out_specs=pl.BlockSpec((tm, D_POOL), lambda b, t: (b * tiles + t, 0)),
        out_shape=jax.ShapeDtypeStruct((batch * seq, D_POOL), BF16),
        scratch_shapes=[pltpu.VMEM((POOL_HALO + tm, D_POOL), F32), stage, stage],
        compiler_params=_params(("arbitrary", "arbitrary")),
        name="prompt_pool",
    )(h, h, w_pool_b, pool_scale)


def _sample_pool_kernel(u_ref, state_ref, wp_ref, scale_ref, o_ref, new_state_ref, full_ref, *, batch, t):
    full_ref[:, 1:HALO, :] = state_ref[...]
    full_ref[:, HALO:, :] = u_ref[...].reshape(batch, t, D_POOL)
    for g, w in enumerate(POOL_WINDOWS):
        cols = slice(g * POOL_GROUP_W, (g + 1) * POOL_GROUP_W)
        wsum = full_ref[:, HALO:, cols]
        for s in range(1, w):
            wsum = wsum + full_ref[:, HALO - s:HALO - s + t, cols]
        d = wsum / float(w) - full_ref[:, HALO:, cols]
        y = jnp.dot(d.reshape(batch * t, POOL_GROUP_W).astype(BF16), wp_ref[g],
                    preferred_element_type=F32)
        o_ref[:, cols] = (y * scale_ref[:, cols]).astype(BF16)
    new_state_ref[...] = full_ref[:, HALO + t - POOL_HIST:, :]


def _sample_pool(h, state, w_pool_b, pool_scale, batch, t):
    full = lambda shape: pl.BlockSpec(shape, lambda i: (0,) * len(shape))
    return pl.pallas_call(
        functools.partial(_sample_pool_kernel, batch=batch, t=t),
        grid=(1,),
        in_specs=[pl.BlockSpec((batch * t, D_POOL), lambda i: (0, U_SEG)),
                  full((batch, POOL_HIST, D_POOL)),
                  full((len(POOL_WINDOWS), POOL_GROUP_W, POOL_GROUP_W)), full((1, D_POOL))],
        out_specs=[full((batch * t, D_POOL)), full((batch, POOL_HIST, D_POOL))],
        out_shape=[jax.ShapeDtypeStruct((batch * t, D_POOL), BF16),
                   jax.ShapeDtypeStruct((batch, POOL_HIST, D_POOL), F32)],
        scratch_shapes=[pltpu.VMEM((batch, HALO + t, D_POOL), F32)],
        compiler_params=_params(("arbitrary",)),
        name="sample_pool",
    )(h, state, w_pool_b, pool_scale)


LN_ROWS = 128


def _layer_norm_rows(h, g, b):
    mu = jnp.mean(h, axis=-1, keepdims=True)
    c = h - mu
    var = jnp.mean(c * c, axis=-1, keepdims=True)
    return c * lax.rsqrt(var + LN_EPS) * g + b


N_STAGE = 2


def _chunk_rows(c, base=0):
    return pl.ds(pl.multiple_of(base + c * LN_ROWS, LN_ROWS), LN_ROWS)


def _layer_norm_to_hbm(src_ref, g_ref, b_ref, dst_hbm, row0, stage_ref, sem, after_chunk=None):
    n_chunks = src_ref.shape[0] // LN_ROWS
    assert n_chunks * LN_ROWS == src_ref.shape[0] and n_chunks % N_STAGE == 0

    def copy(c, slot):
        return pltpu.make_async_copy(stage_ref.at[slot], dst_hbm.at[_chunk_rows(c, row0), :], sem.at[slot])

    def body(c2, carry):
        for slot in range(N_STAGE):
            c = c2 * N_STAGE + slot

            @pl.when(c2 > 0)
            def _():
                copy(c - N_STAGE, slot).wait()

            stage_ref[slot] = _layer_norm_rows(src_ref[_chunk_rows(c), :], g_ref[...], b_ref[...])
            copy(c, slot).start()
            if after_chunk is not None:
                after_chunk(c)
        return carry

    lax.fori_loop(0, n_chunks // N_STAGE, body, 0)
    for slot in range(N_STAGE):
        copy(n_chunks - N_STAGE + slot, slot).wait()


def _stage_scratch():
    return [pltpu.VMEM((N_STAGE, LN_ROWS, D_MODEL), F32), pltpu.SemaphoreType.DMA((N_STAGE,))]


def _out_proj_kernel(att_ref, pool_ref, x_ref, w_ref, g_ref, b_ref, o_hbm, *rest, n_blocks, tn):
    *maybe_wb_ref, h_ref, stage_ref, out_sem = rest
    i = pl.program_id(0)
    j = pl.program_id(1)
    if maybe_wb_ref:
        wb_ref, = maybe_wb_ref
        wb_ref[...] = w_ref[...].astype(BF16)
    else:
        wb_ref = w_ref
    acc = jnp.dot(att_ref[...], wb_ref[:D_ATT, :], preferred_element_type=F32)
    acc = acc + jnp.dot(pool_ref[...], wb_ref[D_ATT:, :], preferred_element_type=F32)
    h = DEEPNORM_ALPHA * x_ref[...] + acc
    for n in range(n_blocks):
        @pl.when(j == n)
        def _():
            h_ref[:, n * tn:(n + 1) * tn] = h

    @pl.when(j == n_blocks - 1)
    def _():
        _layer_norm_to_hbm(h_ref, g_ref, b_ref, o_hbm, i * h_ref.shape[0], stage_ref, out_sem)


def _out_proj_ln(att_o, pool_o, x, w, g, b, tm, tn):
    m = x.shape[0]
    n_blocks = D_MODEL // tn
    cast_weights = w.dtype == F32
    blocked = pl.BlockSpec((None, D_MODEL, tn), lambda i, j: (j, 0, 0))
    out_specs = [pl.BlockSpec(memory_space=pl.ANY)]
    out_shape = [jax.ShapeDtypeStruct((m, D_MODEL), F32)]
    if cast_weights:
        assert m == tm
        w_spec = pl.BlockSpec((D_MODEL, tn), lambda i, j: (0, j))
        out_specs.append(blocked)
        out_shape.append(jax.ShapeDtypeStruct((n_blocks, D_MODEL, tn), BF16))
    else:
        w_spec = blocked
    return pl.pallas_call(
        functools.partial(_out_proj_kernel, n_blocks=n_blocks, tn=tn),
        grid=(m // tm, n_blocks),
        in_specs=[
            pl.BlockSpec((tm, D_ATT), lambda i, j: (i, 0)),
            pl.BlockSpec((tm, D_POOL), lambda i, j: (i, 0)),
            pl.BlockSpec((tm, tn), lambda i, j: (i, j)),
            w_spec,
            pl.BlockSpec((1, D_MODEL), lambda i, j: (0, 0)),
            pl.BlockSpec((1, D_MODEL), lambda i, j: (0, 0)),
        ],
        out_specs=out_specs,
        out_shape=out_shape,
        scratch_shapes=[pltpu.VMEM((tm, D_MODEL), F32)] + _stage_scratch(),
        compiler_params=_params(("arbitrary", "arbitrary")),
        name="out_proj_ln",
    )(att_o, pool_o, x, w, g, b)


DOWN_COLS = 512
SPLIT_LAG = 2


def _ffn_kernel(x_hbm, wg_ref, wu_ref, wd_ref, g_ref, b_ref, y_hbm, acc_ref, xb_ref, in_sem, stage_ref,
                out_sem, *, n_blocks, tm, n_tiles):
    i = pl.program_id(0)
    j = pl.program_id(1)
    n_chunks = tm // LN_ROWS

    def x_copy(tile, c):
        return pltpu.make_async_copy(x_hbm.at[_chunk_rows(c, tile * tm), :], acc_ref.at[_chunk_rows(c), :],
                                     in_sem.at[c])

    def split(tile, c):
        x_copy(tile, c).wait()
        x = acc_ref[_chunk_rows(c), :]
        xb_ref[_chunk_rows(c), :] = x.astype(BF16)
        acc_ref[_chunk_rows(c), :] = DEEPNORM_ALPHA * x

    @pl.when(j == 0)
    def _():
        @pl.when(i == 0)
        def _():
            for c in range(n_chunks):
                x_copy(0, c).start()

        def split_rest(c, carry):
            split(i, c)
            return carry

        lax.fori_loop(jnp.where(i == 0, 0, n_chunks - SPLIT_LAG), n_chunks, split_rest, 0)

    xb = xb_ref[...]
    gate = jnp.dot(xb, wg_ref[...], preferred_element_type=F32)
    up = jnp.dot(xb, wu_ref[...], preferred_element_type=F32)
    act = (gate * (1.0 / (1.0 + jnp.exp(-gate))) * up).astype(BF16)
    for n in range(0, D_MODEL, DOWN_COLS):
        cols = slice(n, n + DOWN_COLS)
        acc_ref[:, cols] += jnp.dot(act, wd_ref[:, cols], preferred_element_type=F32)

    @pl.when(j == n_blocks - 1)
    def _():
        def prefetch(c):
            @pl.when(i + 1 < n_tiles)
            def _():
                x_copy(i + 1, c).start()

                @pl.when(c >= SPLIT_LAG)
                def _():
                    split(i + 1, c - SPLIT_LAG)

        _layer_norm_to_hbm(acc_ref, g_ref, b_ref, y_hbm, i * tm, stage_ref, out_sem, after_chunk=prefetch)


def _ffn_ln(x1, wg_blocks, wu_blocks, wd_b, g, b, tm):
    m = x1.shape[0]
    n_blocks, _, tf = wg_blocks.shape
    n_tiles = m // tm
    n_chunks = tm // LN_ROWS
    assert n_tiles * tm == m and n_chunks * LN_ROWS == tm
    return pl.pallas_call(
        functools.partial(_ffn_kernel, n_blocks=n_blocks, tm=tm, n_tiles=n_tiles),
        grid=(n_tiles, n_blocks),
        in_specs=[
            pl.BlockSpec(memory_space=pl.ANY),
            pl.BlockSpec((None, D_MODEL, tf), lambda i, j: (j, 0, 0)),
            pl.BlockSpec((None, D_MODEL, tf), lambda i, j: (j, 0, 0)),
            pl.BlockSpec((tf, D_MODEL), lambda i, j: (j, 0)),
            pl.BlockSpec((1, D_MODEL), lambda i, j: (0, 0)),
            pl.BlockSpec((1, D_MODEL), lambda i, j: (0, 0)),
        ],
        out_specs=pl.BlockSpec(memory_space=pl.ANY),
        out_shape=jax.ShapeDtypeStruct((m, D_MODEL), F32),
        scratch_shapes=[pltpu.VMEM((tm, D_MODEL), F32), pltpu.VMEM((tm, D_MODEL), BF16),
                        pltpu.SemaphoreType.DMA((n_chunks,))] + _stage_scratch(),
        compiler_params=_params(("arbitrary", "arbitrary")),
        name="ffn_ln",
    )(x1, wg_blocks, wu_blocks, wd_b, g, b)


IN_PROJ_COLS = 512
OUT_PROJ_COLS = 512
FFN_COLS = 256
ROW_TILE = 1024
W_DOWN_CAST_ROWS = 256
POOL_ROWS = 512


def kernel(x_prompt, x_sample, cache_k, cache_v, state_pool, w_in, rel_bias, w_pool, pool_scale,
           w_out, ln1_g, ln1_b, w_gate, w_up, w_down, ln2_g, ln2_b):
    batch, seq, _ = x_prompt.shape
    dec_batch, dec_seq, _ = x_sample.shape
    depth = w_in.shape[0]
    assert depth == 1 and seq % GROUP_ROWS == 0 and seq >= ATT_WINDOW
    lc = cache_k.shape[2]
    n_dec = dec_batch * dec_seq

    n_groups = len(POOL_WINDOWS)
    w_pool_b = _cast_rows(w_pool[0].reshape(n_groups * POOL_GROUP_W, POOL_GROUP_W),
                          n_groups * POOL_GROUP_W).reshape(n_groups, POOL_GROUP_W, POOL_GROUP_W)
    bias_rows = _bias_rows(rel_bias[0])
    g1, b1 = ln1_g[0][None], ln1_b[0][None]
    g2, b2 = ln2_g[0][None], ln2_b[0][None]
    scale = pool_scale[0][None]
    xp = x_prompt.reshape(batch * seq, D_MODEL)
    xs = x_sample.reshape(n_dec, D_MODEL)

    hs, w_in_b = _in_proj_cast(xs, w_in[0], IN_PROJ_COLS)
    hp, w_gate_b, w_up_b = _in_proj(xp, w_in_b, w_gate[0], w_up[0], ROW_TILE, FFN_COLS)

    att_s = _sample_attention(hs, cache_k[0].reshape(dec_batch, lc * N_HEADS, HEAD_DIM),
                              cache_v[0].reshape(dec_batch, lc * N_HEADS, HEAD_DIM),
                              bias_rows, dec_batch, dec_seq)
    pool_s, ps_new = _sample_pool(hs, state_pool[0], w_pool_b, scale, dec_batch, dec_seq)
    x1s, w_out_b = _out_proj_ln(att_s, pool_s, xs, w_out[0], g1, b1, n_dec, OUT_PROJ_COLS)

    att_p, w_down_b, kp_new, vp_new = _prompt_attention(hp, bias_rows, w_down[0], batch, seq,
                                                        W_DOWN_CAST_ROWS)
    pool_p = _prompt_pool(hp, w_pool_b, scale, batch, seq, POOL_ROWS)
    x1p, = _out_proj_ln(att_p, pool_p, xp, w_out_b, g1, b1, ROW_TILE, OUT_PROJ_COLS)

    ys = _ffn_ln(x1s, w_gate_b, w_up_b, w_down_b, g2, b2, n_dec)
    yp = _ffn_ln(x1p, w_gate_b, w_up_b, w_down_b, g2, b2, ROW_TILE)

    hp3 = hp.reshape(batch, seq, 4 * D_ATT)
    dec_shape = (1, dec_batch, dec_seq, N_HEADS, HEAD_DIM)
    return (yp.reshape(batch, seq, D_MODEL),
            ys.reshape(dec_batch, dec_seq, D_MODEL),
            kp_new,
            vp_new,
            hp3[None, :, seq - POOL_HIST:, 3 * D_ATT:],
            hs[:, D_ATT:2 * D_ATT].reshape(dec_shape),
            hs[:, 2 * D_ATT:3 * D_ATT].reshape(dec_shape),
            ps_new[None])
```

```python
import functools

import jax
import jax.numpy as jnp
from jax import lax
from jax.experimental import pallas as pl
from jax.experimental.pallas import tpu as pltpu

D_MODEL = 4096
CHUNK = 64
N_PREV_CHUNKS = 8
ATT_WINDOW = N_PREV_CHUNKS * CHUNK
D_ATT = D_MODEL // 2
D_POOL = D_MODEL - D_ATT
N_HEADS = 16
HEAD_DIM = D_ATT // N_HEADS
REL_CLIP = 256
POOL_WINDOWS = (2, 4, 8, 16)
POOL_GROUP_W = D_POOL // len(POOL_WINDOWS)
POOL_HIST = max(POOL_WINDOWS) - 1
LN_EPS = 1e-5
DEEPNORM_ALPHA = 2.0 ** 0.25
NEG_INF = -1e30
ATT_SCALE = HEAD_DIM ** -0.5
LOG2_E = 1.4426950408889634
Q_SCALE = ATT_SCALE * LOG2_E

GROUP_CHUNKS = 4
GROUP_ROWS = GROUP_CHUNKS * CHUNK
BAND_COLS = GROUP_ROWS + ATT_WINDOW
BIAS_PERIOD = 1024

VMEM_LIMIT_BYTES = 56 * 1024 * 1024

F32 = jnp.float32
BF16 = jnp.bfloat16


def _params(semantics):
    return pltpu.CompilerParams(dimension_semantics=semantics, vmem_limit_bytes=VMEM_LIMIT_BYTES)


def _cast_kernel(w_ref, o_ref):
    o_ref[...] = w_ref[...].astype(BF16)


def _cast_rows(w, rows):
    r, c = w.shape
    return pl.pallas_call(
        _cast_kernel,
        grid=(r // rows,),
        in_specs=[pl.BlockSpec((rows, c), lambda i: (i, 0))],
        out_specs=pl.BlockSpec((rows, c), lambda i: (i, 0)),
        out_shape=jax.ShapeDtypeStruct((r, c), BF16),
        compiler_params=_params(("arbitrary",)),
        name="cast_rows",
    )(w)


def _cast_to_col_blocks(w_ref, o_ref):
    nb, _, tf = o_ref.shape
    for c in range(nb):
        o_ref[c] = w_ref[:, c * tf:(c + 1) * tf].astype(BF16)


def _project(xb_ref, w, o_ref, j, q_blocks):
    acc = jnp.dot(xb_ref[...], w, preferred_element_type=F32)
    o_ref[...] = acc * jnp.where(j < q_blocks, Q_SCALE, 1.0)


def _in_proj_cast_kernel(x_ref, w_ref, o_ref, wb_ref, xb_ref, *, q_blocks):
    j = pl.program_id(0)

    @pl.when(j == 0)
    def _():
        xb_ref[...] = x_ref[...].astype(BF16)

    wb_ref[...] = w_ref[...].astype(BF16)
    _project(xb_ref, wb_ref[...], o_ref, j, q_blocks)


def _in_proj_cast(x, w, tn):
    m = x.shape[0]
    n = w.shape[1]
    return pl.pallas_call(
        functools.partial(_in_proj_cast_kernel, q_blocks=D_ATT // tn),
        grid=(n // tn,),
        in_specs=[
            pl.BlockSpec((m, D_MODEL), lambda j: (0, 0), pipeline_mode=pl.Buffered(1)),
            pl.BlockSpec((D_MODEL, tn), lambda j: (0, j)),
        ],
        out_specs=[pl.BlockSpec((m, tn), lambda j: (0, j)),
                   pl.BlockSpec((None, D_MODEL, tn), lambda j: (j, 0, 0))],
        out_shape=[jax.ShapeDtypeStruct((m, n), F32),
                   jax.ShapeDtypeStruct((n // tn, D_MODEL, tn), BF16)],
        scratch_shapes=[pltpu.VMEM((m, D_MODEL), BF16)],
        compiler_params=_params(("arbitrary",)),
        name="in_proj_cast",
    )(x, w)


X_CHUNK = 128
X_SLOTS = 2


def _in_proj_kernel(x_hbm, w_ref, wg_ref, wu_ref, o_ref, wgb_ref, wub_ref, xb_ref, xs_ref, x_sem,
                    *, q_blocks, tm, n_tiles):
    i = pl.program_id(0)
    j = pl.program_id(1)
    n_chunks = tm // X_CHUNK

    def fetch(tile, c):
        src = x_hbm.at[pl.ds(pl.multiple_of(tile * tm + c * X_CHUNK, X_CHUNK), X_CHUNK), :]
        return pltpu.make_async_copy(src, xs_ref.at[c % X_SLOTS], x_sem.at[c % X_SLOTS])

    def convert(tile, c):
        fetch(tile, c).wait()
        rows = pl.ds(pl.multiple_of(c * X_CHUNK, X_CHUNK), X_CHUNK)
        xb_ref[tile % 2, rows, :] = xs_ref[c % X_SLOTS].astype(BF16)

    @pl.when((i == 0) & (j == 0))
    def _():
        for c in range(X_SLOTS):
            fetch(0, c).start()

        def first_tile(c, carry):
            convert(0, c)

            @pl.when(c + X_SLOTS < n_chunks)
            def _():
                fetch(0, c + X_SLOTS).start()
            return carry

        lax.fori_loop(0, n_chunks, first_tile, 0)

    @pl.when(i + 1 < n_tiles)
    def _():
        @pl.when((j >= 1) & (j <= n_chunks))
        def _():
            convert(i + 1, j - 1)

        @pl.when(j < n_chunks)
        def _():
            fetch(i + 1, j).start()

    _project(xb_ref.at[i % 2], w_ref[...], o_ref, j, q_blocks)
    _cast_to_col_blocks(wg_ref, wgb_ref)
    _cast_to_col_blocks(wu_ref, wub_ref)


def _in_proj(x, w_blocks, w_gate, w_up, tm, tf):
    m = x.shape[0]
    nb, _, tn = w_blocks.shape
    d_ff = w_gate.shape[1]
    n_tiles = m // tm
    steps = n_tiles * nb
    slab = D_MODEL // steps
    assert slab * steps == D_MODEL and slab % 16 == 0 and d_ff % tf == 0
    n_chunks = tm // X_CHUNK
    assert n_tiles * tm == m and n_chunks * X_CHUNK == tm and X_SLOTS <= n_chunks < nb
    side_in = lambda: pl.BlockSpec((slab, d_ff), lambda i, j: (i * nb + j, 0))
    side_out = lambda: pl.BlockSpec((d_ff // tf, slab, tf), lambda i, j: (0, i * nb + j, 0))
    side_shape = jax.ShapeDtypeStruct((d_ff // tf, D_MODEL, tf), BF16)
    return pl.pallas_call(
        functools.partial(_in_proj_kernel, q_blocks=D_ATT // tn, tm=tm, n_tiles=n_tiles),
        grid=(n_tiles, nb),
        in_specs=[
            pl.BlockSpec(memory_space=pl.ANY),
            pl.BlockSpec((None, D_MODEL, tn), lambda i, j: (j, 0, 0)),
            side_in(), side_in(),
        ],
        out_specs=[pl.BlockSpec((tm, tn), lambda i, j: (i, j)), side_out(), side_out()],
        out_shape=[jax.ShapeDtypeStruct((m, nb * tn), F32), side_shape, side_shape],
        scratch_shapes=[pltpu.VMEM((2, tm, D_MODEL), BF16), pltpu.VMEM((X_SLOTS, X_CHUNK, D_MODEL), F32),
                        pltpu.SemaphoreType.DMA((X_SLOTS,))],
        compiler_params=_params(("arbitrary", "arbitrary")),
        name="in_proj",
    )(x, w_blocks, w_gate, w_up)


def _bias_rows(table):
    far = table[:, 2 * REL_CLIP:]
    head = jnp.broadcast_to(far, (table.shape[0], ATT_WINDOW - REL_CLIP))
    tail = jnp.broadcast_to(far, (table.shape[0], BIAS_PERIOD - (ATT_WINDOW + REL_CLIP + 1)))
    return (LOG2_E * jnp.concatenate([head, table[:, ::-1], tail], axis=1))[:, None, :]


def _toeplitz_bias(r, rows):
    return pltpu.roll(jnp.broadcast_to(r, (rows, BIAS_PERIOD)), 0, 1, stride=1, stride_axis=0)


def _prompt_attn_kernel(r_ref, q_ref, k_ref, v_ref, wd_ref, o_ref, wdb_ref, kn_hbm, vn_hbm,
                        bias_ref, qb_ref, kb_ref, vb_ref, new_sem, *, seq, wd_blocks):
    keep = kn_hbm.shape[2]
    hd, b = pl.program_id(0), pl.program_id(1)
    new_copies = [
        pltpu.make_async_copy(src.at[pl.ds(seq - keep, keep), :], dst.at[0, b, :, hd, :], new_sem.at[n])
        for n, (src, dst) in enumerate(((k_ref, kn_hbm), (v_ref, vn_hbm)))]
    for cp in new_copies:
        cp.start()

    @pl.when(hd * pl.num_programs(1) + b < wd_blocks)
    def _():
        wdb_ref[...] = wd_ref[...].astype(BF16)

    @pl.when(b == 0)
    def _():
        t = _toeplitz_bias(r_ref[...], GROUP_ROWS)[:, :BAND_COLS]
        i = lax.broadcasted_iota(jnp.int32, (GROUP_ROWS, BAND_COLS), 0)
        j = lax.broadcasted_iota(jnp.int32, (GROUP_ROWS, BAND_COLS), 1)
        gap = j // CHUNK - i // CHUNK
        bias_ref[...] = jnp.where((gap >= 0) & (gap <= N_PREV_CHUNKS), t, NEG_INF)

    @pl.when((hd == 0) & (b == 0))
    def _():
        lane = lax.broadcasted_iota(jnp.int32, (seq, HEAD_DIM), 1)
        vb_ref[:, HEAD_DIM:] = jnp.where(lane == 0, 1.0, 0.0).astype(BF16)

    qb_ref[...] = q_ref[...].astype(BF16)
    kb_ref[...] = k_ref[...].astype(BF16)
    vb_ref[:, :HEAD_DIM] = v_ref[...].astype(BF16)
    for g in range(seq // GROUP_ROWS):
        q0 = g * GROUP_ROWS
        k0 = max(q0 - ATT_WINDOW, 0)
        nk = q0 + GROUP_ROWS - k0
        c0 = BAND_COLS - nk
        s = lax.dot_general(qb_ref[pl.ds(q0, GROUP_ROWS), :], kb_ref[pl.ds(k0, nk), :],
                            (((1,), (1,)), ((), ())), preferred_element_type=F32)
        s = s + bias_ref[:, c0:]
        m = jnp.max(s, axis=-1, keepdims=True)
        p = jnp.exp2(s - m).astype(BF16)
        o = jnp.dot(p, vb_ref[pl.ds(k0, nk), :], preferred_element_type=F32)
        o_ref[pl.ds(q0, GROUP_ROWS), :] = (o[:, :HEAD_DIM] / o[:, HEAD_DIM:HEAD_DIM + 1]).astype(BF16)
    for cp in new_copies:
        cp.wait()


def _prompt_attention(h, bias_rows, w_down, batch, seq, wd_rows):
    keep = min(ATT_WINDOW, seq)
    new_rows = lambda: pl.BlockSpec(memory_space=pl.ANY)
    new_shape = jax.ShapeDtypeStruct((1, batch, keep, N_HEADS, HEAD_DIM), F32)
    d_ff = w_down.shape[0]
    wd_blocks = d_ff // wd_rows
    assert wd_blocks * wd_rows == d_ff and wd_blocks <= N_HEADS * batch

    def col(seg):
        return pl.BlockSpec((seq, HEAD_DIM), lambda hd, b: (b, seg * N_HEADS + hd))

    wd_spec = lambda: pl.BlockSpec((wd_rows, D_MODEL),
                                   lambda hd, b: (jnp.minimum(hd * batch + b, wd_blocks - 1), 0))
    return pl.pallas_call(
        functools.partial(_prompt_attn_kernel, seq=seq, wd_blocks=wd_blocks),
        grid=(N_HEADS, batch),
        in_specs=[pl.BlockSpec((None, 1, BIAS_PERIOD), lambda hd, b: (hd, 0, 0)), col(0), col(1), col(2),
                  wd_spec()],
        out_specs=[pl.BlockSpec((seq, HEAD_DIM), lambda hd, b: (b, hd)), wd_spec(), new_rows(), new_rows()],
        out_shape=[jax.ShapeDtypeStruct((batch * seq, D_ATT), BF16),
                   jax.ShapeDtypeStruct((d_ff, D_MODEL), BF16), new_shape, new_shape],
        scratch_shapes=[pltpu.VMEM((GROUP_ROWS, BAND_COLS), F32)]
        + [pltpu.VMEM((seq, HEAD_DIM), BF16)] * 2 + [pltpu.VMEM((seq, 2 * HEAD_DIM), BF16)]
        + [pltpu.SemaphoreType.DMA((2,))],
        compiler_params=_params(("arbitrary", "arbitrary")),
        name="prompt_attention",
    )(bias_rows, h, h, h, w_down)


def _sample_attn_kernel(r_ref, q_ref, kn_ref, vn_ref, ck_ref, cv_ref, o_ref, bias_ref, *, lc, t):
    @pl.when(pl.program_id(0) == 0)
    def _():
        for h in range(N_HEADS):
            bias_ref[h] = _toeplitz_bias(r_ref[h], t)

    contract_last = (((1,), (1,)), ((), ()))
    for h in range(N_HEADS):
        cols = slice(h * HEAD_DIM, (h + 1) * HEAD_DIM)
        q = q_ref[:, cols].astype(BF16)
        kc = ck_ref[pl.ds(h, lc, stride=N_HEADS), :].astype(BF16)
        vc = cv_ref[pl.ds(h, lc, stride=N_HEADS), :].astype(BF16)
        kn = kn_ref[:, cols].astype(BF16)
        vn = vn_ref[:, cols].astype(BF16)
        s_c = lax.dot_general(q, kc, contract_last, preferred_element_type=F32) + bias_ref[h, :, :lc]
        s_n = lax.dot_general(q, kn, contract_last, preferred_element_type=F32) + bias_ref[h, :, lc:lc + t]
        m = jnp.maximum(jnp.max(s_c, axis=-1, keepdims=True), jnp.max(s_n, axis=-1, keepdims=True))
        p_c = jnp.exp2(s_c - m)
        p_n = jnp.exp2(s_n - m)
        l = jnp.sum(p_c, axis=-1, keepdims=True) + jnp.sum(p_n, axis=-1, keepdims=True)
        o = (jnp.dot(p_c.astype(BF16), vc, preferred_element_type=F32)
             + jnp.dot(p_n.astype(BF16), vn, preferred_element_type=F32))
        o_ref[:, cols] = (o / l).astype(BF16)


def _sample_attention(h, cache_k, cache_v, bias_rows, batch, t):
    lc = cache_k.shape[1] // N_HEADS
    assert lc == ATT_WINDOW and t <= CHUNK
    row = lambda seg: pl.BlockSpec((t, D_ATT), lambda b: (b, seg))
    cache = lambda: pl.BlockSpec((None, lc * N_HEADS, HEAD_DIM), lambda b: (b, 0, 0))
    return pl.pallas_call(
        functools.partial(_sample_attn_kernel, lc=lc, t=t),
        grid=(batch,),
        in_specs=[pl.BlockSpec((N_HEADS, 1, BIAS_PERIOD), lambda b: (0, 0, 0)),
                  row(0), row(1), row(2), cache(), cache()],
        out_specs=pl.BlockSpec((t, D_ATT), lambda b: (b, 0)),
        out_shape=jax.ShapeDtypeStruct((batch * t, D_ATT), BF16),
        scratch_shapes=[pltpu.VMEM((N_HEADS, t, BIAS_PERIOD), F32)],
        compiler_params=_params(("arbitrary",)),
        name="sample_attention",
    )(bias_rows, h, h, h, cache_k, cache_v)


HALO = POOL_HIST + 1
U_SEG = 3 * D_ATT // D_POOL
POOL_STAGES = max(POOL_WINDOWS).bit_length() - 1
POOL_HALO = 8 * POOL_STAGES


def _prompt_pool_kernel(u_ref, halo_ref, wp_ref, scale_ref, o_ref, full_ref, ping_ref, pong_ref, *, tm):
    t = pl.program_id(1)

    @pl.when(t == 0)
    def _():
        full_ref[0:POOL_HALO, :] = jnp.zeros((POOL_HALO, D_POOL), F32)

    @pl.when(t > 0)
    def _():
        full_ref[0:POOL_HALO, :] = halo_ref[...]

    full_ref[POOL_HALO:, :] = u_ref[...]
    rows = POOL_HALO + tm
    pos = t * tm + lax.broadcasted_iota(jnp.int32, (tm, 1), 0)
    for g, w in enumerate(POOL_WINDOWS):
        cols = slice(g * POOL_GROUP_W, (g + 1) * POOL_GROUP_W)
        src, src_cols = full_ref, cols
        for k in range(1, w.bit_length()):
            lo, back = 8 * k, 2 ** (k - 1)
            dst = (ping_ref, pong_ref)[k % 2]
            dst[lo:, :] = src[lo:, src_cols] + src[lo - back:rows - back, src_cols]
            src, src_cols = dst, slice(None)
        wsum = src[POOL_HALO:, src_cols]
        cnt = jnp.minimum(w, pos + 1).astype(F32)
        d = wsum / cnt - full_ref[POOL_HALO:, cols]
        y = jnp.dot(d.astype(BF16), wp_ref[g], preferred_element_type=F32)
        o_ref[:, cols] = (y * scale_ref[:, cols]).astype(BF16)


def _prompt_pool(h, w_pool_b, pool_scale, batch, seq, tm):
    tiles = seq // tm
    halo_per_tile = tm // POOL_HALO
    assert halo_per_tile * POOL_HALO == tm and POOL_HALO >= POOL_HIST
    stage = pltpu.VMEM((POOL_HALO + tm, POOL_GROUP_W), F32)
    return pl.pallas_call(
        functools.partial(_prompt_pool_kernel, tm=tm),
        grid=(batch, tiles),
        in_specs=[
            pl.BlockSpec((tm, D_POOL), lambda b, t: (b * tiles + t, U_SEG)),
            pl.BlockSpec((POOL_HALO, D_POOL),
                         lambda b, t: (jnp.maximum((b * tiles + t) * halo_per_tile - 1, 0), U_SEG)),
            pl.BlockSpec((len(POOL_WINDOWS), POOL_GROUP_W, POOL_GROUP_W), lambda b, t: (0, 0, 0)),
            pl.BlockSpec((1, D_POOL), lambda b, t: (0, 0)),
        ],
        out_specs=pl.BlockSpec((tm, D_POOL), lambda b, t: (b * tiles + t, 0)),
        out_shape=jax.ShapeDtypeStruct((batch * seq, D_POOL), BF16),
        scratch_shapes=[pltpu.VMEM((POOL_HALO + tm, D_POOL), F32), stage, stage],
        compiler_params=_params(("arbitrary", "arbitrary")),
        name="prompt_pool",
    )(h, h, w_pool_b, pool_scale)


def _sample_pool_kernel(u_ref, state_ref, wp_ref, scale_ref, o_ref, new_state_ref, full_ref, *, batch, t):
    full_ref[:, 1:HALO, :] = state_ref[...]
    full_ref[:, HALO:, :] = u_ref[...].reshape(batch, t, D_POOL)
    for g, w in enumerate(POOL_WINDOWS):
        cols = slice(g * POOL_GROUP_W, (g + 1) * POOL_GROUP_W)
        wsum = full_ref[:, HALO:, cols]
        for s in range(1, w):
            wsum = wsum + full_ref[:, HALO - s:HALO - s + t, cols]
        d = wsum / float(w) - full_ref[:, HALO:, cols]
        y = jnp.dot(d.reshape(batch * t, POOL_GROUP_W).astype(BF16), wp_ref[g],
                    preferred_element_type=F32)
        o_ref[:, cols] = (y * scale_ref[:, cols]).astype(BF16)
    new_state_ref[...] = full_ref[:, HALO + t - POOL_HIST:, :]


def _sample_pool(h, state, w_pool_b, pool_scale, batch, t):
    full = lambda shape: pl.BlockSpec(shape, lambda i: (0,) * len(shape))
    return pl.pallas_call(
        functools.partial(_sample_pool_kernel, batch=batch, t=t),
        grid=(1,),
        in_specs=[pl.BlockSpec((batch * t, D_POOL), lambda i: (0, U_SEG)),
                  full((batch, POOL_HIST, D_POOL)),
                  full((len(POOL_WINDOWS), POOL_GROUP_W, POOL_GROUP_W)), full((1, D_POOL))],
        out_specs=[full((batch * t, D_POOL)), full((batch, POOL_HIST, D_POOL))],
        out_shape=[jax.ShapeDtypeStruct((batch * t, D_POOL), BF16),
                   jax.ShapeDtypeStruct((batch, POOL_HIST, D_POOL), F32)],
        scratch_shapes=[pltpu.VMEM((batch, HALO + t, D_POOL), F32)],
        compiler_params=_params(("arbitrary",)),
        name="sample_pool",
    )(h, state, w_pool_b, pool_scale)


LN_ROWS = 128


def _layer_norm_rows(h, g, b):
    mu = jnp.mean(h, axis=-1, keepdims=True)
    c = h - mu
    var = jnp.mean(c * c, axis=-1, keepdims=True)
    return c * lax.rsqrt(var + LN_EPS) * g + b


N_STAGE = 2


def _chunk_rows(c, base=0):
    return pl.ds(pl.multiple_of(base + c * LN_ROWS, LN_ROWS), LN_ROWS)


def _layer_norm_to_hbm(src_ref, g_ref, b_ref, dst_hbm, row0, stage_ref, sem, after_chunk=None):
    n_chunks = src_ref.shape[0] // LN_ROWS
    assert n_chunks * LN_ROWS == src_ref.shape[0] and n_chunks % N_STAGE == 0

    def copy(c, slot):
        return pltpu.make_async_copy(stage_ref.at[slot], dst_hbm.at[_chunk_rows(c, row0), :], sem.at[slot])

    def body(c2, carry):
        for slot in range(N_STAGE):
            c = c2 * N_STAGE + slot

            @pl.when(c2 > 0)
            def _():
                copy(c - N_STAGE, slot).wait()

            stage_ref[slot] = _layer_norm_rows(src_ref[_chunk_rows(c), :], g_ref[...], b_ref[...])
            copy(c, slot).start()
            if after_chunk is not None:
                after_chunk(c)
        return carry

    lax.fori_loop(0, n_chunks // N_STAGE, body, 0)
    for slot in range(N_STAGE):
        copy(n_chunks - N_STAGE + slot, slot).wait()


def _stage_scratch():
    return [pltpu.VMEM((N_STAGE, LN_ROWS, D_MODEL), F32), pltpu.SemaphoreType.DMA((N_STAGE,))]


def _out_proj_kernel(att_ref, pool_ref, x_ref, w_ref, o_ref, *maybe_wb_ref):
    if maybe_wb_ref:
        wb_ref, = maybe_wb_ref
        wb_ref[...] = w_ref[...].astype(BF16)
    else:
        wb_ref = w_ref
    acc = jnp.dot(att_ref[...], wb_ref[:D_ATT, :], preferred_element_type=F32)
    acc = acc + jnp.dot(pool_ref[...], wb_ref[D_ATT:, :], preferred_element_type=F32)
    o_ref[...] = DEEPNORM_ALPHA * x_ref[...] + acc


def _out_proj(att_o, pool_o, x, w, tm, tn):
    m = x.shape[0]
    n_blocks = D_MODEL // tn
    cast_weights = w.dtype == F32
    blocked = pl.BlockSpec((None, D_MODEL, tn), lambda i, j: (j, 0, 0))
    out_specs = [pl.BlockSpec((tm, tn), lambda i, j: (i, j))]
    out_shape = [jax.ShapeDtypeStruct((m, D_MODEL), F32)]
    if cast_weights:
        assert m == tm
        w_spec = pl.BlockSpec((D_MODEL, tn), lambda i, j: (0, j))
        out_specs.append(blocked)
        out_shape.append(jax.ShapeDtypeStruct((n_blocks, D_MODEL, tn), BF16))
    else:
        w_spec = blocked
    return pl.pallas_call(
        _out_proj_kernel,
        grid=(m // tm, n_blocks),
        in_specs=[
            pl.BlockSpec((tm, D_ATT), lambda i, j: (i, 0)),
            pl.BlockSpec((tm, D_POOL), lambda i, j: (i, 0)),
            pl.BlockSpec((tm, tn), lambda i, j: (i, j)),
            w_spec,
        ],
        out_specs=out_specs,
        out_shape=out_shape,
        compiler_params=_params(("arbitrary", "arbitrary")),
        name="out_proj",
    )(att_o, pool_o, x, w)


DOWN_COLS = 512
SPLIT_LAG = 2


def _ffn_kernel(h_hbm, wg_ref, wu_ref, wd_ref, g1_ref, b1_ref, g_ref, b_ref, y_hbm, acc_ref, xb_ref, in_sem,
                stage_ref, out_sem, *, n_blocks, tm, n_tiles):
    i = pl.program_id(0)
    j = pl.program_id(1)
    n_chunks = tm // LN_ROWS

    def x_copy(tile, c):
        return pltpu.make_async_copy(h_hbm.at[_chunk_rows(c, tile * tm), :], acc_ref.at[_chunk_rows(c), :],
                                     in_sem.at[c])

    def split(tile, c):
        x_copy(tile, c).wait()
        x = _layer_norm_rows(acc_ref[_chunk_rows(c), :], g1_ref[...], b1_ref[...])
        xb_ref[_chunk_rows(c), :] = x.astype(BF16)
        acc_ref[_chunk_rows(c), :] = DEEPNORM_ALPHA * x

    @pl.when(j == 0)
    def _():
        @pl.when(i == 0)
        def _():
            for c in range(n_chunks):
                x_copy(0, c).start()

        def split_rest(c, carry):
            split(i, c)
            return carry

        lax.fori_loop(jnp.where(i == 0, 0, n_chunks - SPLIT_LAG), n_chunks, split_rest, 0)

    xb = xb_ref[...]
    gate = jnp.dot(xb, wg_ref[...], preferred_element_type=F32)
    up = jnp.dot(xb, wu_ref[...], preferred_element_type=F32)
    act = (gate * (1.0 / (1.0 + jnp.exp(-gate))) * up).astype(BF16)
    for n in range(0, D_MODEL, DOWN_COLS):
        cols = slice(n, n + DOWN_COLS)
        acc_ref[:, cols] += jnp.dot(act, wd_ref[:, cols], preferred_element_type=F32)

    @pl.when(j == n_blocks - 1)
    def _():
        def prefetch(c):
            @pl.when(i + 1 < n_tiles)
            def _():
                x_copy(i + 1, c).start()

                @pl.when(c >= SPLIT_LAG)
                def _():
                    split(i + 1, c - SPLIT_LAG)

        _layer_norm_to_hbm(acc_ref, g_ref, b_ref, y_hbm, i * tm, stage_ref, out_sem, after_chunk=prefetch)


def _ffn_ln(h, wg_blocks, wu_blocks, wd_b, g1, b1, g2, b2, tm):
    m = h.shape[0]
    n_blocks, _, tf = wg_blocks.shape
    n_tiles = m // tm
    n_chunks = tm // LN_ROWS
    assert n_tiles * tm == m and n_chunks * LN_ROWS == tm
    return pl.pallas_call(
        functools.partial(_ffn_kernel, n_blocks=n_blocks, tm=tm, n_tiles=n_tiles),
        grid=(n_tiles, n_blocks),
        in_specs=[
            pl.BlockSpec(memory_space=pl.ANY),
            pl.BlockSpec((None, D_MODEL, tf), lambda i, j: (j, 0, 0)),
            pl.BlockSpec((None, D_MODEL, tf), lambda i, j: (j, 0, 0)),
            pl.BlockSpec((tf, D_MODEL), lambda i, j: (j, 0)),
        ] + [pl.BlockSpec((1, D_MODEL), lambda i, j: (0, 0))] * 4,
        out_specs=pl.BlockSpec(memory_space=pl.ANY),
        out_shape=jax.ShapeDtypeStruct((m, D_MODEL), F32),
        scratch_shapes=[pltpu.VMEM((tm, D_MODEL), F32), pltpu.VMEM((tm, D_MODEL), BF16),
                        pltpu.SemaphoreType.DMA((n_chunks,))] + _stage_scratch(),
        compiler_params=_params(("arbitrary", "arbitrary")),
        name="ffn_ln",
    )(h, wg_blocks, wu_blocks, wd_b, g1, b1, g2, b2)


IN_PROJ_COLS = 512
OUT_PROJ_COLS = 1024
FFN_COLS = 256
ROW_TILE = 1024
W_DOWN_CAST_ROWS = 256
POOL_ROWS = 512


def kernel(x_prompt, x_sample, cache_k, cache_v, state_pool, w_in, rel_bias, w_pool, pool_scale,
           w_out, ln1_g, ln1_b, w_gate, w_up, w_down, ln2_g, ln2_b):
    batch, seq, _ = x_prompt.shape
    dec_batch, dec_seq, _ = x_sample.shape
    depth = w_in.shape[0]
    assert depth == 1 and seq % GROUP_ROWS == 0 and seq >= ATT_WINDOW
    lc = cache_k.shape[2]
    n_dec = dec_batch * dec_seq

    n_groups = len(POOL_WINDOWS)
    w_pool_b = _cast_rows(w_pool[0].reshape(n_groups * POOL_GROUP_W, POOL_GROUP_W),
                          n_groups * POOL_GROUP_W).reshape(n_groups, POOL_GROUP_W, POOL_GROUP_W)
    bias_rows = _bias_rows(rel_bias[0])
    g1, b1 = ln1_g[0][None], ln1_b[0][None]
    g2, b2 = ln2_g[0][None], ln2_b[0][None]
    scale = pool_scale[0][None]
    xp = x_prompt.reshape(batch * seq, D_MODEL)
    xs = x_sample.reshape(n_dec, D_MODEL)

    hs, w_in_b = _in_proj_cast(xs, w_in[0], IN_PROJ_COLS)
    hp, w_gate_b, w_up_b = _in_proj(xp, w_in_b, w_gate[0], w_up[0], ROW_TILE, FFN_COLS)

    att_s = _sample_attention(hs, cache_k[0].reshape(dec_batch, lc * N_HEADS, HEAD_DIM),
                              cache_v[0].reshape(dec_batch, lc * N_HEADS, HEAD_DIM),
                              bias_rows, dec_batch, dec_seq)
    pool_s, ps_new = _sample_pool(hs, state_pool[0], w_pool_b, scale, dec_batch, dec_seq)
    res_s, w_out_b = _out_proj(att_s, pool_s, xs, w_out[0], n_dec, OUT_PROJ_COLS)

    att_p, w_down_b, kp_new, vp_new = _prompt_attention(hp, bias_rows, w_down[0], batch, seq,
                                                        W_DOWN_CAST_ROWS)
    pool_p = _prompt_pool(hp, w_pool_b, scale, batch, seq, POOL_ROWS)
    res_p, = _out_proj(att_p, pool_p, xp, w_out_b, ROW_TILE, OUT_PROJ_COLS)

    ys = _ffn_ln(res_s, w_gate_b, w_up_b, w_down_b, g1, b1, g2, b2, n_dec)
    yp = _ffn_ln(res_p, w_gate_b, w_up_b, w_down_b, g1, b1, g2, b2, ROW_TILE)

    hp3 = hp.reshape(batch, seq, 4 * D_ATT)
    dec_shape = (1, dec_batch, dec_seq, N_HEADS, HEAD_DIM)
    return (yp.reshape(batch, seq, D_MODEL),
            ys.reshape(dec_batch, dec_seq, D_MODEL),
            kp_new,
            vp_new,
            hp3[None, :, seq - POOL_HIST:, 3 * D_ATT:],
            hs[:, D_ATT:2 * D_ATT].reshape(dec_shape),
            hs[:, 2 * D_ATT:3 * D_ATT].reshape(dec_shape),
            ps_new[None])
```

```python
import functools

import jax
import jax.numpy as jnp
from jax import lax
from jax.experimental import pallas as pl
from jax.experimental.pallas import tpu as pltpu

D_MODEL = 4096
CHUNK = 64
N_PREV_CHUNKS = 8
ATT_WINDOW = N_PREV_CHUNKS * CHUNK
D_ATT = D_MODEL // 2
D_POOL = D_MODEL - D_ATT
N_HEADS = 16
HEAD_DIM = D_ATT // N_HEADS
REL_CLIP = 256
POOL_WINDOWS = (2, 4, 8, 16)
POOL_GROUP_W = D_POOL // len(POOL_WINDOWS)
POOL_HIST = max(POOL_WINDOWS) - 1
LN_EPS = 1e-5
DEEPNORM_ALPHA = 2.0 ** 0.25
NEG_INF = -1e30
ATT_SCALE = HEAD_DIM ** -0.5
LOG2_E = 1.4426950408889634
Q_SCALE = ATT_SCALE * LOG2_E

GROUP_CHUNKS = 4
GROUP_ROWS = GROUP_CHUNKS * CHUNK
BAND_COLS = GROUP_ROWS + ATT_WINDOW
BIAS_PERIOD = 1024

VMEM_LIMIT_BYTES = 56 * 1024 * 1024

F32 = jnp.float32
BF16 = jnp.bfloat16


def _params(semantics):
    return pltpu.CompilerParams(dimension_semantics=semantics, vmem_limit_bytes=VMEM_LIMIT_BYTES)


def _cast_kernel(w_ref, o_ref):
    o_ref[...] = w_ref[...].astype(BF16)


def _cast_rows(w, rows):
    r, c = w.shape
    return pl.pallas_call(
        _cast_kernel,
        grid=(r // rows,),
        in_specs=[pl.BlockSpec((rows, c), lambda i: (i, 0))],
        out_specs=pl.BlockSpec((rows, c), lambda i: (i, 0)),
        out_shape=jax.ShapeDtypeStruct((r, c), BF16),
        compiler_params=_params(("arbitrary",)),
        name="cast_rows",
    )(w)


def _cast_to_col_blocks(w_ref, o_ref):
    nb, _, tf = o_ref.shape
    for c in range(nb):
        o_ref[c] = w_ref[:, c * tf:(c + 1) * tf].astype(BF16)


def _project(xb_ref, w, o_ref, j, q_blocks):
    acc = jnp.dot(xb_ref[...], w, preferred_element_type=F32)
    o_ref[...] = acc * jnp.where(j < q_blocks, Q_SCALE, 1.0)


def _in_proj_cast_kernel(x_ref, w_ref, o_ref, wb_ref, xb_ref, *, q_blocks):
    j = pl.program_id(0)

    @pl.when(j == 0)
    def _():
        xb_ref[...] = x_ref[...].astype(BF16)

    wb_ref[...] = w_ref[...].astype(BF16)
    _project(xb_ref, wb_ref[...], o_ref, j, q_blocks)


def _in_proj_cast(x, w, tn):
    m = x.shape[0]
    n = w.shape[1]
    return pl.pallas_call(
        functools.partial(_in_proj_cast_kernel, q_blocks=D_ATT // tn),
        grid=(n // tn,),
        in_specs=[
            pl.BlockSpec((m, D_MODEL), lambda j: (0, 0), pipeline_mode=pl.Buffered(1)),
            pl.BlockSpec((D_MODEL, tn), lambda j: (0, j)),
        ],
        out_specs=[pl.BlockSpec((m, tn), lambda j: (0, j)),
                   pl.BlockSpec((None, D_MODEL, tn), lambda j: (j, 0, 0))],
        out_shape=[jax.ShapeDtypeStruct((m, n), F32),
                   jax.ShapeDtypeStruct((n // tn, D_MODEL, tn), BF16)],
        scratch_shapes=[pltpu.VMEM((m, D_MODEL), BF16)],
        compiler_params=_params(("arbitrary",)),
        name="in_proj_cast",
    )(x, w)


X_CHUNK = 128
X_SLOTS = 2


def _in_proj_kernel(x_hbm, w_ref, wg_ref, wu_ref, o_ref, wgb_ref, wub_ref, xb_ref, xs_ref, x_sem,
                    *, q_blocks, tm, n_tiles):
    i = pl.program_id(0)
    j = pl.program_id(1)
    n_chunks = tm // X_CHUNK

    def fetch(tile, c):
        src = x_hbm.at[pl.ds(pl.multiple_of(tile * tm + c * X_CHUNK, X_CHUNK), X_CHUNK), :]
        return pltpu.make_async_copy(src, xs_ref.at[c % X_SLOTS], x_sem.at[c % X_SLOTS])

    def convert(tile, c):
        fetch(tile, c).wait()
        rows = pl.ds(pl.multiple_of(c * X_CHUNK, X_CHUNK), X_CHUNK)
        xb_ref[tile % 2, rows, :] = xs_ref[c % X_SLOTS].astype(BF16)

    @pl.when((i == 0) & (j == 0))
    def _():
        for c in range(X_SLOTS):
            fetch(0, c).start()

        def first_tile(c, carry):
            convert(0, c)

            @pl.when(c + X_SLOTS < n_chunks)
            def _():
                fetch(0, c + X_SLOTS).start()
            return carry

        lax.fori_loop(0, n_chunks, first_tile, 0)

    @pl.when(i + 1 < n_tiles)
    def _():
        @pl.when((j >= 1) & (j <= n_chunks))
        def _():
            convert(i + 1, j - 1)

        @pl.when(j < n_chunks)
        def _():
            fetch(i + 1, j).start()

    _project(xb_ref.at[i % 2], w_ref[...], o_ref, j, q_blocks)
    _cast_to_col_blocks(wg_ref, wgb_ref)
    _cast_to_col_blocks(wu_ref, wub_ref)


def _in_proj(x, w_blocks, w_gate, w_up, tm, tf):
    m = x.shape[0]
    nb, _, tn = w_blocks.shape
    d_ff = w_gate.shape[1]
    n_tiles = m // tm
    steps = n_tiles * nb
    slab = D_MODEL // steps
    assert slab * steps == D_MODEL and slab % 16 == 0 and d_ff % tf == 0
    n_chunks = tm // X_CHUNK
    assert n_tiles * tm == m and n_chunks * X_CHUNK == tm and X_SLOTS <= n_chunks < nb
    side_in = lambda: pl.BlockSpec((slab, d_ff), lambda i, j: (i * nb + j, 0))
    side_out = lambda: pl.BlockSpec((d_ff // tf, slab, tf), lambda i, j: (0, i * nb + j, 0))
    side_shape = jax.ShapeDtypeStruct((d_ff // tf, D_MODEL, tf), BF16)
    return pl.pallas_call(
        functools.partial(_in_proj_kernel, q_blocks=D_ATT // tn, tm=tm, n_tiles=n_tiles),
        grid=(n_tiles, nb),
        in_specs=[
            pl.BlockSpec(memory_space=pl.ANY),
            pl.BlockSpec((None, D_MODEL, tn), lambda i, j: (j, 0, 0)),
            side_in(), side_in(),
        ],
        out_specs=[pl.BlockSpec((tm, tn), lambda i, j: (i, j)), side_out(), side_out()],
        out_shape=[jax.ShapeDtypeStruct((m, nb * tn), F32), side_shape, side_shape],
        scratch_shapes=[pltpu.VMEM((2, tm, D_MODEL), BF16), pltpu.VMEM((X_SLOTS, X_CHUNK, D_MODEL), F32),
                        pltpu.SemaphoreType.DMA((X_SLOTS,))],
        compiler_params=_params(("arbitrary", "arbitrary")),
        name="in_proj",
    )(x, w_blocks, w_gate, w_up)


def _bias_rows(table):
    far = table[:, 2 * REL_CLIP:]
    head = jnp.broadcast_to(far, (table.shape[0], ATT_WINDOW - REL_CLIP))
    tail = jnp.broadcast_to(far, (table.shape[0], BIAS_PERIOD - (ATT_WINDOW + REL_CLIP + 1)))
    return (LOG2_E * jnp.concatenate([head, table[:, ::-1], tail], axis=1))[:, None, :]


def _toeplitz_bias(r, rows):
    return pltpu.roll(jnp.broadcast_to(r, (rows, BIAS_PERIOD)), 0, 1, stride=1, stride_axis=0)


def _prompt_attn_kernel(r_ref, q_ref, k_ref, v_ref, wd_ref, o_ref, wdb_ref, kn_hbm, vn_hbm,
                        bias_ref, qb_ref, kb_ref, vb_ref, new_sem, *, seq, wd_blocks):
    keep = kn_hbm.shape[2]
    hd, b = pl.program_id(0), pl.program_id(1)
    new_copies = [
        pltpu.make_async_copy(src.at[pl.ds(seq - keep, keep), :], dst.at[0, b, :, hd, :], new_sem.at[n])
        for n, (src, dst) in enumerate(((k_ref, kn_hbm), (v_ref, vn_hbm)))]
    for cp in new_copies:
        cp.start()

    @pl.when(hd * pl.num_programs(1) + b < wd_blocks)
    def _():
        wdb_ref[...] = wd_ref[...].astype(BF16)

    @pl.when(b == 0)
    def _():
        t = _toeplitz_bias(r_ref[...], GROUP_ROWS)[:, :BAND_COLS]
        i = lax.broadcasted_iota(jnp.int32, (GROUP_ROWS, BAND_COLS), 0)
        j = lax.broadcasted_iota(jnp.int32, (GROUP_ROWS, BAND_COLS), 1)
        gap = j // CHUNK - i // CHUNK
        bias_ref[...] = jnp.where((gap >= 0) & (gap <= N_PREV_CHUNKS), t, NEG_INF)

    @pl.when((hd == 0) & (b == 0))
    def _():
        lane = lax.broadcasted_iota(jnp.int32, (seq, HEAD_DIM), 1)
        vb_ref[:, HEAD_DIM:] = jnp.where(lane == 0, 1.0, 0.0).astype(BF16)

    qb_ref[...] = q_ref[...].astype(BF16)
    kb_ref[...] = k_ref[...].astype(BF16)
    vb_ref[:, :HEAD_DIM] = v_ref[...].astype(BF16)
    for g in range(seq // GROUP_ROWS):
        q0 = g * GROUP_ROWS
        k0 = max(q0 - ATT_WINDOW, 0)
        nk = q0 + GROUP_ROWS - k0
        c0 = BAND_COLS - nk
        s = lax.dot_general(qb_ref[pl.ds(q0, GROUP_ROWS), :], kb_ref[pl.ds(k0, nk), :],
                            (((1,), (1,)), ((), ())), preferred_element_type=F32)
        s = s + bias_ref[:, c0:]
        m = jnp.max(s, axis=-1, keepdims=True)
        p = jnp.exp2(s - m).astype(BF16)
        o = jnp.dot(p, vb_ref[pl.ds(k0, nk), :], preferred_element_type=F32)
        o_ref[pl.ds(q0, GROUP_ROWS), :] = (o[:, :HEAD_DIM] / o[:, HEAD_DIM:HEAD_DIM + 1]).astype(BF16)
    for cp in new_copies:
        cp.wait()


def _prompt_attention(h, bias_rows, w_down, batch, seq, wd_rows):
    keep = min(ATT_WINDOW, seq)
    new_rows = lambda: pl.BlockSpec(memory_space=pl.ANY)
    new_shape = jax.ShapeDtypeStruct((1, batch, keep, N_HEADS, HEAD_DIM), F32)
    d_ff = w_down.shape[0]
    wd_blocks = d_ff // wd_rows
    assert wd_blocks * wd_rows == d_ff and wd_blocks <= N_HEADS * batch

    def col(seg):
        return pl.BlockSpec((seq, HEAD_DIM), lambda hd, b: (b, seg * N_HEADS + hd))

    wd_spec = lambda: pl.BlockSpec((wd_rows, D_MODEL),
                                   lambda hd, b: (jnp.minimum(hd * batch + b, wd_blocks - 1), 0))
    return pl.pallas_call(
        functools.partial(_prompt_attn_kernel, seq=seq, wd_blocks=wd_blocks),
        grid=(N_HEADS, batch),
        in_specs=[pl.BlockSpec((None, 1, BIAS_PERIOD), lambda hd, b: (hd, 0, 0)), col(0), col(1), col(2),
                  wd_spec()],
        out_specs=[pl.BlockSpec((seq, HEAD_DIM), lambda hd, b: (b, hd)), wd_spec(), new_rows(), new_rows()],
        out_shape=[jax.ShapeDtypeStruct((batch * seq, D_ATT), BF16),
                   jax.ShapeDtypeStruct((d_ff, D_MODEL), BF16), new_shape, new_shape],
        scratch_shapes=[pltpu.VMEM((GROUP_ROWS, BAND_COLS), F32)]
        + [pltpu.VMEM((seq, HEAD_DIM), BF16)] * 2 + [pltpu.VMEM((seq, 2 * HEAD_DIM), BF16)]
        + [pltpu.SemaphoreType.DMA((2,))],
        compiler_params=_params(("arbitrary", "arbitrary")),
        name="prompt_attention",
    )(bias_rows, h, h, h, w_down)


def _sample_attn_kernel(r_ref, q_ref, kn_ref, vn_ref, ck_ref, cv_ref, o_ref, bias_ref, *, lc, t):
    @pl.when(pl.program_id(0) == 0)
    def _():
        for h in range(N_HEADS):
            bias_ref[h] = _toeplitz_bias(r_ref[h], t)

    contract_last = (((1,), (1,)), ((), ()))
    for h in range(N_HEADS):
        cols = slice(h * HEAD_DIM, (h + 1) * HEAD_DIM)
        q = q_ref[:, cols].astype(BF16)
        kc = ck_ref[pl.ds(h, lc, stride=N_HEADS), :].astype(BF16)
        vc = cv_ref[pl.ds(h, lc, stride=N_HEADS), :].astype(BF16)
        kn = kn_ref[:, cols].astype(BF16)
        vn = vn_ref[:, cols].astype(BF16)
        s_c = lax.dot_general(q, kc, contract_last, preferred_element_type=F32) + bias_ref[h, :, :lc]
        s_n = lax.dot_general(q, kn, contract_last, preferred_element_type=F32) + bias_ref[h, :, lc:lc + t]
        m = jnp.maximum(jnp.max(s_c, axis=-1, keepdims=True), jnp.max(s_n, axis=-1, keepdims=True))
        p_c = jnp.exp2(s_c - m)
        p_n = jnp.exp2(s_n - m)
        l = jnp.sum(p_c, axis=-1, keepdims=True) + jnp.sum(p_n, axis=-1, keepdims=True)
        o = (jnp.dot(p_c.astype(BF16), vc, preferred_element_type=F32)
             + jnp.dot(p_n.astype(BF16), vn, preferred_element_type=F32))
        o_ref[:, cols] = (o / l).astype(BF16)


def _sample_attention(h, cache_k, cache_v, bias_rows, batch, t):
    lc = cache_k.shape[1] // N_HEADS
    assert lc == ATT_WINDOW and t <= CHUNK
    row = lambda seg: pl.BlockSpec((t, D_ATT), lambda b: (b, seg))
    cache = lambda: pl.BlockSpec((None, lc * N_HEADS, HEAD_DIM), lambda b: (b, 0, 0))
    return pl.pallas_call(
        functools.partial(_sample_attn_kernel, lc=lc, t=t),
        grid=(batch,),
        in_specs=[pl.BlockSpec((N_HEADS, 1, BIAS_PERIOD), lambda b: (0, 0, 0)),
                  row(0), row(1), row(2), cache(), cache()],
        out_specs=pl.BlockSpec((t, D_ATT), lambda b: (b, 0)),
        out_shape=jax.ShapeDtypeStruct((batch * t, D_ATT), BF16),
        scratch_shapes=[pltpu.VMEM((N_HEADS, t, BIAS_PERIOD), F32)],
        compiler_params=_params(("arbitrary",)),
        name="sample_attention",
    )(bias_rows, h, h, h, cache_k, cache_v)


HALO = POOL_HIST + 1
U_SEG = 3 * D_ATT // D_POOL
POOL_STAGES = max(POOL_WINDOWS).bit_length() - 1
POOL_HALO = 8 * POOL_STAGES


def _prompt_pool_kernel(u_ref, halo_ref, wp_ref, scale_ref, o_ref, full_ref, ping_ref, pong_ref, *, tm):
    t = pl.program_id(1)

    @pl.when(t == 0)
    def _():
        full_ref[0:POOL_HALO, :] = jnp.zeros((POOL_HALO, D_POOL), F32)

    @pl.when(t > 0)
    def _():
        full_ref[0:POOL_HALO, :] = halo_ref[...]

    full_ref[POOL_HALO:, :] = u_ref[...]
    rows = POOL_HALO + tm
    pos = t * tm + lax.broadcasted_iota(jnp.int32, (tm, 1), 0)
    for g, w in enumerate(POOL_WINDOWS):
        cols = slice(g * POOL_GROUP_W, (g + 1) * POOL_GROUP_W)
        src, src_cols = full_ref, cols
        for k in range(1, w.bit_length()):
            lo, back = 8 * k, 2 ** (k - 1)
            dst = (ping_ref, pong_ref)[k % 2]
            dst[lo:, :] = src[lo:, src_cols] + src[lo - back:rows - back, src_cols]
            src, src_cols = dst, slice(None)
        wsum = src[POOL_HALO:, src_cols]
        cnt = jnp.minimum(w, pos + 1).astype(F32)
        d = wsum / cnt - full_ref[POOL_HALO:, cols]
        y = jnp.dot(d.astype(BF16), wp_ref[g], preferred_element_type=F32)
        o_ref[:, cols] = (y * scale_ref[:, cols]).astype(BF16)


def _prompt_pool(h, w_pool_b, pool_scale, batch, seq, tm):
    tiles = seq // tm
    halo_per_tile = tm // POOL_HALO
    assert halo_per_tile * POOL_HALO == tm and POOL_HALO >= POOL_HIST
    stage = pltpu.VMEM((POOL_HALO + tm, POOL_GROUP_W), F32)
    return pl.pallas_call(
        functools.partial(_prompt_pool_kernel, tm=tm),
        grid=(batch, tiles),
        in_specs=[
            pl.BlockSpec((tm, D_POOL), lambda b, t: (b * tiles + t, U_SEG)),
            pl.BlockSpec((POOL_HALO, D_POOL),
                         lambda b, t: (jnp.maximum((b * tiles + t) * halo_per_tile - 1, 0), U_SEG)),
            pl.BlockSpec((len(POOL_WINDOWS), POOL_GROUP_W, POOL_GROUP_W), lambda b, t: (0, 0, 0)),
            pl.BlockSpec((1, D_POOL), lambda b, t: (0, 0)),
        ],
        out_specs=pl.BlockSpec((tm, D_POOL), lambda b, t: (b * tiles + t, 0)),
        out_shape=jax.ShapeDtypeStruct((batch * seq, D_POOL), BF16),
        scratch_shapes=[pltpu.VMEM((POOL_HALO + tm, D_POOL), F32), stage, stage],
        compiler_params=_params(("arbitrary", "arbitrary")),
        name="prompt_pool",
    )(h, h, w_pool_b, pool_scale)


def _sample_pool_kernel(u_ref, state_ref, wp_ref, scale_ref, o_ref, new_state_ref, full_ref, *, batch, t):
    full_ref[:, 1:HALO, :] = state_ref[...]
    full_ref[:, HALO:, :] = u_ref[...].reshape(batch, t, D_POOL)
    for g, w in enumerate(POOL_WINDOWS):
        cols = slice(g * POOL_GROUP_W, (g + 1) * POOL_GROUP_W)
        wsum = full_ref[:, HALO:, cols]
        for s in range(1, w):
            wsum = wsum + full_ref[:, HALO - s:HALO - s + t, cols]
        d = wsum / float(w) - full_ref[:, HALO:, cols]
        y = jnp.dot(d.reshape(batch * t, POOL_GROUP_W).astype(BF16), wp_ref[g],
                    preferred_element_type=F32)
        o_ref[:, cols] = (y * scale_ref[:, cols]).astype(BF16)
    new_state_ref[...] = full_ref[:, HALO + t - POOL_HIST:, :]


def _sample_pool(h, state, w_pool_b, pool_scale, batch, t):
    full = lambda shape: pl.BlockSpec(shape, lambda i: (0,) * len(shape))
    return pl.pallas_call(
        functools.partial(_sample_pool_kernel, batch=batch, t=t),
        grid=(1,),
        in_specs=[pl.BlockSpec((batch * t, D_POOL), lambda i: (0, U_SEG)),
                  full((batch, POOL_HIST, D_POOL)),
                  full((len(POOL_WINDOWS), POOL_GROUP_W, POOL_GROUP_W)), full((1, D_POOL))],
        out_specs=[full((batch * t, D_POOL)), full((batch, POOL_HIST, D_POOL))],
        out_shape=[jax.ShapeDtypeStruct((batch * t, D_POOL), BF16),
                   jax.ShapeDtypeStruct((batch, POOL_HIST, D_POOL), F32)],
        scratch_shapes=[pltpu.VMEM((batch, HALO + t, D_POOL), F32)],
        compiler_params=_params(("arbitrary",)),
        name="sample_pool",
    )(h, state, w_pool_b, pool_scale)


LN_ROWS = 128


def _layer_norm_rows(h, g, b):
    mu = jnp.mean(h, axis=-1, keepdims=True)
    c = h - mu
    var = jnp.mean(c * c, axis=-1, keepdims=True)
    return c * lax.rsqrt(var + LN_EPS) * g + b


N_STAGE = 2


def _chunk_rows(c, base=0):
    return pl.ds(pl.multiple_of(base + c * LN_ROWS, LN_ROWS), LN_ROWS)


def _layer_norm_to_hbm(src_ref, g_ref, b_ref, dst_hbm, row0, stage_ref, sem, after_chunk=None):
    n_chunks = src_ref.shape[0] // LN_ROWS
    assert n_chunks * LN_ROWS == src_ref.shape[0] and n_chunks % N_STAGE == 0

    def copy(c, slot):
        return pltpu.make_async_copy(stage_ref.at[slot], dst_hbm.at[_chunk_rows(c, row0), :], sem.at[slot])

    def body(c2, carry):
        for slot in range(N_STAGE):
            c = c2 * N_STAGE + slot

            @pl.when(c2 > 0)
            def _():
                copy(c - N_STAGE, slot).wait()

            stage_ref[slot] = _layer_norm_rows(src_ref[_chunk_rows(c), :], g_ref[...], b_ref[...])
            copy(c, slot).start()
            if after_chunk is not None:
                after_chunk(c)
        return carry

    lax.fori_loop(0, n_chunks // N_STAGE, body, 0)
    for slot in range(N_STAGE):
        copy(n_chunks - N_STAGE + slot, slot).wait()


def _stage_scratch():
    return [pltpu.VMEM((N_STAGE, LN_ROWS, D_MODEL), F32), pltpu.SemaphoreType.DMA((N_STAGE,))]


def _out_proj_kernel(att_ref, pool_ref, x_ref, w_ref, o_ref, *maybe_wb_ref):
    if maybe_wb_ref:
        wb_ref, = maybe_wb_ref
        wb_ref[...] = w_ref[...].astype(BF16)
    else:
        wb_ref = w_ref
    acc = jnp.dot(att_ref[...], wb_ref[:D_ATT, :], preferred_element_type=F32)
    acc = acc + jnp.dot(pool_ref[...], wb_ref[D_ATT:, :], preferred_element_type=F32)
    o_ref[...] = DEEPNORM_ALPHA * x_ref[...] + acc


def _out_proj(att_o, pool_o, x, w, tm, tn):
    m = x.shape[0]
    n_blocks = D_MODEL // tn
    cast_weights = w.dtype == F32
    blocked = pl.BlockSpec((None, D_MODEL, tn), lambda i, j: (j, 0, 0))
    out_specs = [pl.BlockSpec((tm, tn), lambda i, j: (i, j))]
    out_shape = [jax.ShapeDtypeStruct((m, D_MODEL), F32)]
    if cast_weights:
        assert m == tm
        w_spec = pl.BlockSpec((D_MODEL, tn), lambda i, j: (0, j))
        out_specs.append(blocked)
        out_shape.append(jax.ShapeDtypeStruct((n_blocks, D_MODEL, tn), BF16))
    else:
        w_spec = blocked
    return pl.pallas_call(
        _out_proj_kernel,
        grid=(m // tm, n_blocks),
        in_specs=[
            pl.BlockSpec((tm, D_ATT), lambda i, j: (i, 0)),
            pl.BlockSpec((tm, D_POOL), lambda i, j: (i, 0)),
            pl.BlockSpec((tm, tn), lambda i, j: (i, j)),
            w_spec,
        ],
        out_specs=out_specs,
        out_shape=out_shape,
        compiler_params=_params(("arbitrary", "arbitrary")),
        name="out_proj",
    )(att_o, pool_o, x, w)


DOWN_COLS = 512
SPLIT_LAG = 2


def _ffn_kernel(h_hbm, hx_hbm, wg_ref, wu_ref, wd_ref, g1_ref, b1_ref, g_ref, b_ref, y_hbm, yx_hbm,
                acc_ref, xb_ref, in_sem, stage_ref, out_sem, *, n_blocks, tm, tx, n_tiles):
    i = pl.program_id(0)
    j = pl.program_id(1)
    n_chunks = tm // LN_ROWS
    x_chunks = tx // LN_ROWS

    def x_copy(tile, c):
        return pltpu.make_async_copy(h_hbm.at[_chunk_rows(c, tile * tm), :], acc_ref.at[_chunk_rows(c), :],
                                     in_sem.at[c])

    def extra_copy(k):
        return pltpu.make_async_copy(hx_hbm.at[_chunk_rows(k), :], acc_ref.at[_chunk_rows(n_chunks + k), :],
                                     in_sem.at[n_chunks + k])

    def split(copy, c):
        copy.wait()
        x = _layer_norm_rows(acc_ref[_chunk_rows(c), :], g1_ref[...], b1_ref[...])
        xb_ref[_chunk_rows(c), :] = x.astype(BF16)
        acc_ref[_chunk_rows(c), :] = DEEPNORM_ALPHA * x

    @pl.when(j == 0)
    def _():
        @pl.when(i == 0)
        def _():
            for c in range(n_chunks):
                x_copy(0, c).start()
            for k in range(x_chunks):
                extra_copy(k).start()

        def split_rest(c, carry):
            split(x_copy(i, c), c)
            return carry

        lax.fori_loop(jnp.where(i == 0, 0, n_chunks - SPLIT_LAG), n_chunks, split_rest, 0)

        @pl.when(i == 0)
        def _():
            for k in range(x_chunks):
                split(extra_copy(k), n_chunks + k)

    def block_step(rows):
        xb = xb_ref[:rows, :]
        gate = jnp.dot(xb, wg_ref[...], preferred_element_type=F32)
        up = jnp.dot(xb, wu_ref[...], preferred_element_type=F32)
        act = (gate * (1.0 / (1.0 + jnp.exp(-gate))) * up).astype(BF16)
        for n in range(0, D_MODEL, DOWN_COLS):
            cols = slice(n, n + DOWN_COLS)
            acc_ref[:rows, cols] += jnp.dot(act, wd_ref[:, cols], preferred_element_type=F32)

    @pl.when(i == 0)
    def _():
        block_step(tm + tx)

    @pl.when(i > 0)
    def _():
        block_step(tm)

    @pl.when(j == n_blocks - 1)
    def _():
        def prefetch(c):
            @pl.when(i + 1 < n_tiles)
            def _():
                x_copy(i + 1, c).start()

                @pl.when(c >= SPLIT_LAG)
                def _():
                    split(x_copy(i + 1, c - SPLIT_LAG), c - SPLIT_LAG)

        _layer_norm_to_hbm(acc_ref.at[pl.ds(0, tm), :], g_ref, b_ref, y_hbm, i * tm, stage_ref, out_sem,
                           after_chunk=prefetch)

        @pl.when(i == 0)
        def _():
            _layer_norm_to_hbm(acc_ref.at[pl.ds(tm, tx), :], g_ref, b_ref, yx_hbm, 0, stage_ref, out_sem)


def _ffn_ln(h, hx, wg_blocks, wu_blocks, wd_b, g1, b1, g2, b2, tm):
    m, tx = h.shape[0], hx.shape[0]
    n_blocks, _, tf = wg_blocks.shape
    n_tiles = m // tm
    n_chunks, x_chunks = tm // LN_ROWS, tx // LN_ROWS
    assert n_tiles * tm == m and n_chunks * LN_ROWS == tm and x_chunks * LN_ROWS == tx
    assert SPLIT_LAG <= n_chunks
    return pl.pallas_call(
        functools.partial(_ffn_kernel, n_blocks=n_blocks, tm=tm, tx=tx, n_tiles=n_tiles),
        grid=(n_tiles, n_blocks),
        in_specs=[
            pl.BlockSpec(memory_space=pl.ANY),
            pl.BlockSpec(memory_space=pl.ANY),
            pl.BlockSpec((None, D_MODEL, tf), lambda i, j: (j, 0, 0)),
            pl.BlockSpec((None, D_MODEL, tf), lambda i, j: (j, 0, 0)),
            pl.BlockSpec((tf, D_MODEL), lambda i, j: (j, 0)),
        ] + [pl.BlockSpec((1, D_MODEL), lambda i, j: (0, 0))] * 4,
        out_specs=[pl.BlockSpec(memory_space=pl.ANY), pl.BlockSpec(memory_space=pl.ANY)],
        out_shape=[jax.ShapeDtypeStruct((m, D_MODEL), F32), jax.ShapeDtypeStruct((tx, D_MODEL), F32)],
        scratch_shapes=[pltpu.VMEM((tm + tx, D_MODEL), F32), pltpu.VMEM((tm + tx, D_MODEL), BF16),
                        pltpu.SemaphoreType.DMA((n_chunks + x_chunks,))] + _stage_scratch(),
        compiler_params=_params(("arbitrary", "arbitrary")),
        name="ffn_ln",
    )(h, hx, wg_blocks, wu_blocks, wd_b, g1, b1, g2, b2)


IN_PROJ_COLS = 512
OUT_PROJ_COLS = 1024
FFN_COLS = 256
ROW_TILE = 1024
W_DOWN_CAST_ROWS = 256
POOL_ROWS = 512


def kernel(x_prompt, x_sample, cache_k, cache_v, state_pool, w_in, rel_bias, w_pool, pool_scale,
           w_out, ln1_g, ln1_b, w_gate, w_up, w_down, ln2_g, ln2_b):
    batch, seq, _ = x_prompt.shape
    dec_batch, dec_seq, _ = x_sample.shape
    depth = w_in.shape[0]
    assert depth == 1 and seq % GROUP_ROWS == 0 and seq >= ATT_WINDOW
    lc = cache_k.shape[2]
    n_dec = dec_batch * dec_seq

    n_groups = len(POOL_WINDOWS)
    w_pool_b = _cast_rows(w_pool[0].reshape(n_groups * POOL_GROUP_W, POOL_GROUP_W),
                          n_groups * POOL_GROUP_W).reshape(n_groups, POOL_GROUP_W, POOL_GROUP_W)
    bias_rows = _bias_rows(rel_bias[0])
    g1, b1 = ln1_g[0][None], ln1_b[0][None]
    g2, b2 = ln2_g[0][None], ln2_b[0][None]
    scale = pool_scale[0][None]
    xp = x_prompt.reshape(batch * seq, D_MODEL)
    xs = x_sample.reshape(n_dec, D_MODEL)

    hs, w_in_b = _in_proj_cast(xs, w_in[0], IN_PROJ_COLS)
    hp, w_gate_b, w_up_b = _in_proj(xp, w_in_b, w_gate[0], w_up[0], ROW_TILE, FFN_COLS)

    att_s = _sample_attention(hs, cache_k[0].reshape(dec_batch, lc * N_HEADS, HEAD_DIM),
                              cache_v[0].reshape(dec_batch, lc * N_HEADS, HEAD_DIM),
                              bias_rows, dec_batch, dec_seq)
    pool_s, ps_new = _sample_pool(hs, state_pool[0], w_pool_b, scale, dec_batch, dec_seq)
    res_s, w_out_b = _out_proj(att_s, pool_s, xs, w_out[0], n_dec, OUT_PROJ_COLS)

    att_p, w_down_b, kp_new, vp_new = _prompt_attention(hp, bias_rows, w_down[0], batch, seq,
                                                        W_DOWN_CAST_ROWS)
    pool_p = _prompt_pool(hp, w_pool_b, scale, batch, seq, POOL_ROWS)
    res_p, = _out_proj(att_p, pool_p, xp, w_out_b, ROW_TILE, OUT_PROJ_COLS)

    yp, ys = _ffn_ln(res_p, res_s, w_gate_b, w_up_b, w_down_b, g1, b1, g2, b2, ROW_TILE)

    hp3 = hp.reshape(batch, seq, 4 * D_ATT)
    dec_shape = (1, dec_batch, dec_seq, N_HEADS, HEAD_DIM)
    return (yp.reshape(batch, seq, D_MODEL),
            ys.reshape(dec_batch, dec_seq, D_MODEL),
            kp_new,
            vp_new,
            hp3[None, :, seq - POOL_HIST:, 3 * D_ATT:],
            hs[:, D_ATT:2 * D_ATT].reshape(dec_shape),
            hs[:, 2 * D_ATT:3 * D_ATT].reshape(dec_shape),
            ps_new[None])
```

```python
import functools

import jax
import jax.numpy as jnp
from jax import lax
from jax.experimental import pallas as pl
from jax.experimental.pallas import tpu as pltpu

D_MODEL = 4096
CHUNK = 64
N_PREV_CHUNKS = 8
ATT_WINDOW = N_PREV_CHUNKS * CHUNK
D_ATT = D_MODEL // 2
D_POOL = D_MODEL - D_ATT
N_HEADS = 16
HEAD_DIM = D_ATT // N_HEADS
REL_CLIP = 256
POOL_WINDOWS = (2, 4, 8, 16)
POOL_GROUP_W = D_POOL // len(POOL_WINDOWS)
POOL_HIST = max(POOL_WINDOWS) - 1
LN_EPS = 1e-5
DEEPNORM_ALPHA = 2.0 ** 0.25
NEG_INF = -1e30
ATT_SCALE = HEAD_DIM ** -0.5
LOG2_E = 1.4426950408889634
Q_SCALE = ATT_SCALE * LOG2_E

GROUP_CHUNKS = 4
GROUP_ROWS = GROUP_CHUNKS * CHUNK
BAND_COLS = GROUP_ROWS + ATT_WINDOW
BIAS_PERIOD = 1024

VMEM_LIMIT_BYTES = 56 * 1024 * 1024

F32 = jnp.float32
BF16 = jnp.bfloat16


def _params(semantics):
    return pltpu.CompilerParams(dimension_semantics=semantics, vmem_limit_bytes=VMEM_LIMIT_BYTES)


def _cast_kernel(w_ref, o_ref):
    o_ref[...] = w_ref[...].astype(BF16)


def _cast_rows(w, rows):
    r, c = w.shape
    return pl.pallas_call(
        _cast_kernel,
        grid=(r // rows,),
        in_specs=[pl.BlockSpec((rows, c), lambda i: (i, 0))],
        out_specs=pl.BlockSpec((rows, c), lambda i: (i, 0)),
        out_shape=jax.ShapeDtypeStruct((r, c), BF16),
        compiler_params=_params(("arbitrary",)),
        name="cast_rows",
    )(w)


def _cast_to_col_blocks(w_ref, o_ref):
    nb, _, tf = o_ref.shape
    for c in range(nb):
        o_ref[c] = w_ref[:, c * tf:(c + 1) * tf].astype(BF16)


def _project(xb_ref, w, o_ref, j, q_blocks):
    acc = jnp.dot(xb_ref[...], w, preferred_element_type=F32)
    o_ref[...] = acc * jnp.where(j < q_blocks, Q_SCALE, 1.0)


def _in_proj_cast_kernel(x_ref, w_ref, o_ref, wb_ref, xb_ref, *, q_blocks):
    j = pl.program_id(0)

    @pl.when(j == 0)
    def _():
        xb_ref[...] = x_ref[...].astype(BF16)

    wb_ref[...] = w_ref[...].astype(BF16)
    _project(xb_ref, wb_ref[...], o_ref, j, q_blocks)


def _in_proj_cast(x, w, tn):
    m = x.shape[0]
    n = w.shape[1]
    return pl.pallas_call(
        functools.partial(_in_proj_cast_kernel, q_blocks=D_ATT // tn),
        grid=(n // tn,),
        in_specs=[
            pl.BlockSpec((m, D_MODEL), lambda j: (0, 0), pipeline_mode=pl.Buffered(1)),
            pl.BlockSpec((D_MODEL, tn), lambda j: (0, j)),
        ],
        out_specs=[pl.BlockSpec((m, tn), lambda j: (0, j)),
                   pl.BlockSpec((None, D_MODEL, tn), lambda j: (j, 0, 0))],
        out_shape=[jax.ShapeDtypeStruct((m, n), F32),
                   jax.ShapeDtypeStruct((n // tn, D_MODEL, tn), BF16)],
        scratch_shapes=[pltpu.VMEM((m, D_MODEL), BF16)],
        compiler_params=_params(("arbitrary",)),
        name="in_proj_cast",
    )(x, w)


X_CHUNK = 128
X_SLOTS = 2


def _in_proj_kernel(x_hbm, w_ref, wg_ref, wu_ref, o_ref, wgb_ref, wub_ref, xb_ref, xs_ref, x_sem,
                    *, q_blocks, tm, n_tiles):
    i = pl.program_id(0)
    j = pl.program_id(1)
    n_chunks = tm // X_CHUNK

    def fetch(tile, c):
        src = x_hbm.at[pl.ds(pl.multiple_of(tile * tm + c * X_CHUNK, X_CHUNK), X_CHUNK), :]
        return pltpu.make_async_copy(src, xs_ref.at[c % X_SLOTS], x_sem.at[c % X_SLOTS])

    def convert(tile, c):
        fetch(tile, c).wait()
        rows = pl.ds(pl.multiple_of(c * X_CHUNK, X_CHUNK), X_CHUNK)
        xb_ref[tile % 2, rows, :] = xs_ref[c % X_SLOTS].astype(BF16)

    @pl.when((i == 0) & (j == 0))
    def _():
        for c in range(X_SLOTS):
            fetch(0, c).start()

        def first_tile(c, carry):
            convert(0, c)

            @pl.when(c + X_SLOTS < n_chunks)
            def _():
                fetch(0, c + X_SLOTS).start()
            return carry

        lax.fori_loop(0, n_chunks, first_tile, 0)

    @pl.when(i + 1 < n_tiles)
    def _():
        @pl.when((j >= 1) & (j <= n_chunks))
        def _():
            convert(i + 1, j - 1)

        @pl.when(j < n_chunks)
        def _():
            fetch(i + 1, j).start()

    _project(xb_ref.at[i % 2], w_ref[...], o_ref, j, q_blocks)
    _cast_to_col_blocks(wg_ref, wgb_ref)
    _cast_to_col_blocks(wu_ref, wub_ref)


def _in_proj(x, w_blocks, w_gate, w_up, tm, tf):
    m = x.shape[0]
    nb, _, tn = w_blocks.shape
    d_ff = w_gate.shape[1]
    n_tiles = m // tm
    steps = n_tiles * nb
    slab = D_MODEL // steps
    assert slab * steps == D_MODEL and slab % 16 == 0 and d_ff % tf == 0
    n_chunks = tm // X_CHUNK
    assert n_tiles * tm == m and n_chunks * X_CHUNK == tm and X_SLOTS <= n_chunks < nb
    side_in = lambda: pl.BlockSpec((slab, d_ff), lambda i, j: (i * nb + j, 0))
    side_out = lambda: pl.BlockSpec((d_ff // tf, slab, tf), lambda i, j: (0, i * nb + j, 0))
    side_shape = jax.ShapeDtypeStruct((d_ff // tf, D_MODEL, tf), BF16)
    return pl.pallas_call(
        functools.partial(_in_proj_kernel, q_blocks=D_ATT // tn, tm=tm, n_tiles=n_tiles),
        grid=(n_tiles, nb),
        in_specs=[
            pl.BlockSpec(memory_space=pl.ANY),
            pl.BlockSpec((None, D_MODEL, tn), lambda i, j: (j, 0, 0)),
            side_in(), side_in(),
        ],
        out_specs=[pl.BlockSpec((tm, tn), lambda i, j: (i, j)), side_out(), side_out()],
        out_shape=[jax.ShapeDtypeStruct((m, nb * tn), F32), side_shape, side_shape],
        scratch_shapes=[pltpu.VMEM((2, tm, D_MODEL), BF16), pltpu.VMEM((X_SLOTS, X_CHUNK, D_MODEL), F32),
                        pltpu.SemaphoreType.DMA((X_SLOTS,))],
        compiler_params=_params(("arbitrary", "arbitrary")),
        name="in_proj",
    )(x, w_blocks, w_gate, w_up)


def _bias_rows(table):
    far = table[:, 2 * REL_CLIP:]
    head = jnp.broadcast_to(far, (table.shape[0], ATT_WINDOW - REL_CLIP))
    tail = jnp.broadcast_to(far, (table.shape[0], BIAS_PERIOD - (ATT_WINDOW + REL_CLIP + 1)))
    return (LOG2_E * jnp.concatenate([head, table[:, ::-1], tail], axis=1))[:, None, :]


def _toeplitz_bias(r, rows):
    return pltpu.roll(jnp.broadcast_to(r, (rows, BIAS_PERIOD)), 0, 1, stride=1, stride_axis=0)


def _prompt_attn_kernel(r_ref, q_ref, k_ref, v_ref, wd_ref, o_ref, wdb_ref, kn_hbm, vn_hbm,
                        bias_ref, qb_ref, kb_ref, vb_ref, new_sem, *, seq, wd_blocks):
    keep = kn_hbm.shape[2]
    hd, b = pl.program_id(0), pl.program_id(1)
    new_copies = [
        pltpu.make_async_copy(src.at[pl.ds(seq - keep, keep), :], dst.at[0, b, :, hd, :], new_sem.at[n])
        for n, (src, dst) in enumerate(((k_ref, kn_hbm), (v_ref, vn_hbm)))]
    for cp in new_copies:
        cp.start()

    @pl.when(hd * pl.num_programs(1) + b < wd_blocks)
    def _():
        wdb_ref[...] = wd_ref[...].astype(BF16)

    @pl.when(b == 0)
    def _():
        t = _toeplitz_bias(r_ref[...], GROUP_ROWS)[:, :BAND_COLS]
        i = lax.broadcasted_iota(jnp.int32, (GROUP_ROWS, BAND_COLS), 0)
        j = lax.broadcasted_iota(jnp.int32, (GROUP_ROWS, BAND_COLS), 1)
        gap = j // CHUNK - i // CHUNK
        bias_ref[...] = jnp.where((gap >= 0) & (gap <= N_PREV_CHUNKS), t, NEG_INF)

    @pl.when((hd == 0) & (b == 0))
    def _():
        lane = lax.broadcasted_iota(jnp.int32, (seq, HEAD_DIM), 1)
        vb_ref[:, HEAD_DIM:] = jnp.where(lane == 0, 1.0, 0.0).astype(BF16)

    qb_ref[...] = q_ref[...].astype(BF16)
    kb_ref[...] = k_ref[...].astype(BF16)
    vb_ref[:, :HEAD_DIM] = v_ref[...].astype(BF16)
    for g in range(seq // GROUP_ROWS):
        q0 = g * GROUP_ROWS
        k0 = max(q0 - ATT_WINDOW, 0)
        nk = q0 + GROUP_ROWS - k0
        c0 = BAND_COLS - nk
        s = lax.dot_general(qb_ref[pl.ds(q0, GROUP_ROWS), :], kb_ref[pl.ds(k0, nk), :],
                            (((1,), (1,)), ((), ())), preferred_element_type=F32)
        s = s + bias_ref[:, c0:]
        m = jnp.max(s, axis=-1, keepdims=True)
        p = jnp.exp2(s - m).astype(BF16)
        o = jnp.dot(p, vb_ref[pl.ds(k0, nk), :], preferred_element_type=F32)
        o_ref[pl.ds(q0, GROUP_ROWS), :] = (o[:, :HEAD_DIM] / o[:, HEAD_DIM:HEAD_DIM + 1]).astype(BF16)
    for cp in new_copies:
        cp.wait()


def _prompt_attention(h, bias_rows, w_down, batch, seq, wd_rows):
    keep = min(ATT_WINDOW, seq)
    new_rows = lambda: pl.BlockSpec(memory_space=pl.ANY)
    new_shape = jax.ShapeDtypeStruct((1, batch, keep, N_HEADS, HEAD_DIM), F32)
    d_ff = w_down.shape[0]
    wd_blocks = d_ff // wd_rows
    assert wd_blocks * wd_rows == d_ff and wd_blocks <= N_HEADS * batch

    def col(seg):
        return pl.BlockSpec((seq, HEAD_DIM), lambda hd, b: (b, seg * N_HEADS + hd))

    wd_spec = lambda: pl.BlockSpec((wd_rows, D_MODEL),
                                   lambda hd, b: (jnp.minimum(hd * batch + b, wd_blocks - 1), 0))
    return pl.pallas_call(
        functools.partial(_prompt_attn_kernel, seq=seq, wd_blocks=wd_blocks),
        grid=(N_HEADS, batch),
        in_specs=[pl.BlockSpec((None, 1, BIAS_PERIOD), lambda hd, b: (hd, 0, 0)), col(0), col(1), col(2),
                  wd_spec()],
        out_specs=[pl.BlockSpec((seq, HEAD_DIM), lambda hd, b: (b, hd)), wd_spec(), new_rows(), new_rows()],
        out_shape=[jax.ShapeDtypeStruct((batch * seq, D_ATT), BF16),
                   jax.ShapeDtypeStruct((d_ff, D_MODEL), BF16), new_shape, new_shape],
        scratch_shapes=[pltpu.VMEM((GROUP_ROWS, BAND_COLS), F32)]
        + [pltpu.VMEM((seq, HEAD_DIM), BF16)] * 2 + [pltpu.VMEM((seq, 2 * HEAD_DIM), BF16)]
        + [pltpu.SemaphoreType.DMA((2,))],
        compiler_params=_params(("arbitrary", "arbitrary")),
        name="prompt_attention",
    )(bias_rows, h, h, h, w_down)


def _sample_attn_kernel(r_ref, q_ref, kn_ref, vn_ref, ck_ref, cv_ref, o_ref, bias_ref, *, lc, t):
    @pl.when(pl.program_id(0) == 0)
    def _():
        for h in range(N_HEADS):
            bias_ref[h] = _toeplitz_bias(r_ref[h], t)

    contract_last = (((1,), (1,)), ((), ()))
    for h in range(N_HEADS):
        cols = slice(h * HEAD_DIM, (h + 1) * HEAD_DIM)
        q = q_ref[:, cols].astype(BF16)
        kc = ck_ref[pl.ds(h, lc, stride=N_HEADS), :].astype(BF16)
        vc = cv_ref[pl.ds(h, lc, stride=N_HEADS), :].astype(BF16)
        kn = kn_ref[:, cols].astype(BF16)
        vn = vn_ref[:, cols].astype(BF16)
        s_c = lax.dot_general(q, kc, contract_last, preferred_element_type=F32) + bias_ref[h, :, :lc]
        s_n = lax.dot_general(q, kn, contract_last, preferred_element_type=F32) + bias_ref[h, :, lc:lc + t]
        m = jnp.maximum(jnp.max(s_c, axis=-1, keepdims=True), jnp.max(s_n, axis=-1, keepdims=True))
        p_c = jnp.exp2(s_c - m)
        p_n = jnp.exp2(s_n - m)
        l = jnp.sum(p_c, axis=-1, keepdims=True) + jnp.sum(p_n, axis=-1, keepdims=True)
        o = (jnp.dot(p_c.astype(BF16), vc, preferred_element_type=F32)
             + jnp.dot(p_n.astype(BF16), vn, preferred_element_type=F32))
        o_ref[:, cols] = (o / l).astype(BF16)


def _sample_attention(h, cache_k, cache_v, bias_rows, batch, t):
    lc = cache_k.shape[1] // N_HEADS
    assert lc == ATT_WINDOW and t <= CHUNK
    row = lambda seg: pl.BlockSpec((t, D_ATT), lambda b: (b, seg))
    cache = lambda: pl.BlockSpec((None, lc * N_HEADS, HEAD_DIM), lambda b: (b, 0, 0))
    return pl.pallas_call(
        functools.partial(_sample_attn_kernel, lc=lc, t=t),
        grid=(batch,),
        in_specs=[pl.BlockSpec((N_HEADS, 1, BIAS_PERIOD), lambda b: (0, 0, 0)),
                  row(0), row(1), row(2), cache(), cache()],
        out_specs=pl.BlockSpec((t, D_ATT), lambda b: (b, 0)),
        out_shape=jax.ShapeDtypeStruct((batch * t, D_ATT), BF16),
        scratch_shapes=[pltpu.VMEM((N_HEADS, t, BIAS_PERIOD), F32)],
        compiler_params=_params(("arbitrary",)),
        name="sample_attention",
    )(bias_rows, h, h, h, cache_k, cache_v)


HALO = POOL_HIST + 1
U_SEG = 3 * D_ATT // D_POOL
POOL_STAGES = max(POOL_WINDOWS).bit_length() - 1
POOL_HALO = 8 * POOL_STAGES


def _prompt_pool_kernel(u_ref, halo_ref, wp_ref, scale_ref, o_ref, full_ref, ping_ref, pong_ref, *, tm):
    t = pl.program_id(1)

    @pl.when(t == 0)
    def _():
        full_ref[0:POOL_HALO, :] = jnp.zeros((POOL_HALO, D_POOL), F32)

    @pl.when(t > 0)
    def _():
        full_ref[0:POOL_HALO, :] = halo_ref[...]

    full_ref[POOL_HALO:, :] = u_ref[...]
    rows = POOL_HALO + tm
    pos = t * tm + lax.broadcasted_iota(jnp.int32, (tm, 1), 0)
    for g, w in enumerate(POOL_WINDOWS):
        cols = slice(g * POOL_GROUP_W, (g + 1) * POOL_GROUP_W)
        src, src_cols = full_ref, cols
        for k in range(1, w.bit_length()):
            lo, back = 8 * k, 2 ** (k - 1)
            dst = (ping_ref, pong_ref)[k % 2]
            dst[lo:, :] = src[lo:, src_cols] + src[lo - back:rows - back, src_cols]
            src, src_cols = dst, slice(None)
        wsum = src[POOL_HALO:, src_cols]
        cnt = jnp.minimum(w, pos + 1).astype(F32)
        d = wsum / cnt - full_ref[POOL_HALO:, cols]
        y = jnp.dot(d.astype(BF16), wp_ref[g], preferred_element_type=F32)
        o_ref[:, cols] = (y * scale_ref[:, cols]).astype(BF16)


def _prompt_pool(h, w_pool_b, pool_scale, batch, seq, tm):
    tiles = seq // tm
    halo_per_tile = tm // POOL_HALO
    assert halo_per_tile * POOL_HALO == tm and POOL_HALO >= POOL_HIST
    stage = pltpu.VMEM((POOL_HALO + tm, POOL_GROUP_W), F32)
    return pl.pallas_call(
        functools.partial(_prompt_pool_kernel, tm=tm),
        grid=(batch, tiles),
        in_specs=[
            pl.BlockSpec((tm, D_POOL), lambda b, t: (b * tiles + t, U_SEG)),
            pl.BlockSpec((POOL_HALO, D_POOL),
                         lambda b, t: (jnp.maximum((b * tiles + t) * halo_per_tile - 1, 0), U_SEG)),
            pl.BlockSpec((len(POOL_WINDOWS), POOL_GROUP_W, POOL_GROUP_W), lambda b, t: (0, 0, 0)),
            pl.BlockSpec((1, D_POOL), lambda b, t: (0, 0)),
        ],
        out_specs=pl.BlockSpec((tm, D_POOL), lambda b, t: (b * tiles + t, 0)),
        out_shape=jax.ShapeDtypeStruct((batch * seq, D_POOL), BF16),
        scratch_shapes=[pltpu.VMEM((POOL_HALO + tm, D_POOL), F32), stage, stage],
        compiler_params=_params(("arbitrary", "arbitrary")),
        name="prompt_pool",
    )(h, h, w_pool_b, pool_scale)


def _sample_pool_kernel(u_ref, state_ref, wp_ref, scale_ref, o_ref, new_state_ref, full_ref, *, batch, t):
    full_ref[:, 1:HALO, :] = state_ref[...]
    full_ref[:, HALO:, :] = u_ref[...].reshape(batch, t, D_POOL)
    for g, w in enumerate(POOL_WINDOWS):
        cols = slice(g * POOL_GROUP_W, (g + 1) * POOL_GROUP_W)
        wsum = full_ref[:, HALO:, cols]
        for s in range(1, w):
            wsum = wsum + full_ref[:, HALO - s:HALO - s + t, cols]
        d = wsum / float(w) - full_ref[:, HALO:, cols]
        y = jnp.dot(d.reshape(batch * t, POOL_GROUP_W).astype(BF16), wp_ref[g],
                    preferred_element_type=F32)
        o_ref[:, cols] = (y * scale_ref[:, cols]).astype(BF16)
    new_state_ref[...] = full_ref[:, HALO + t - POOL_HIST:, :]


def _sample_pool(h, state, w_pool_b, pool_scale, batch, t):
    full = lambda shape: pl.BlockSpec(shape, lambda i: (0,) * len(shape))
    return pl.pallas_call(
        functools.partial(_sample_pool_kernel, batch=batch, t=t),
        grid=(1,),
        in_specs=[pl.BlockSpec((batch * t, D_POOL), lambda i: (0, U_SEG)),
                  full((batch, POOL_HIST, D_POOL)),
                  full((len(POOL_WINDOWS), POOL_GROUP_W, POOL_GROUP_W)), full((1, D_POOL))],
        out_specs=[full((batch * t, D_POOL)), full((batch, POOL_HIST, D_POOL))],
        out_shape=[jax.ShapeDtypeStruct((batch * t, D_POOL), BF16),
                   jax.ShapeDtypeStruct((batch, POOL_HIST, D_POOL), F32)],
        scratch_shapes=[pltpu.VMEM((batch, HALO + t, D_POOL), F32)],
        compiler_params=_params(("arbitrary",)),
        name="sample_pool",
    )(h, state, w_pool_b, pool_scale)


LN_ROWS = 128


def _layer_norm_rows(h, g, b):
    mu = jnp.mean(h, axis=-1, keepdims=True)
    c = h - mu
    var = jnp.mean(c * c, axis=-1, keepdims=True)
    return c * lax.rsqrt(var + LN_EPS) * g + b


N_STAGE = 2


def _chunk_rows(c, base=0):
    return pl.ds(pl.multiple_of(base + c * LN_ROWS, LN_ROWS), LN_ROWS)


def _layer_norm_to_hbm(src_ref, g_ref, b_ref, dst_hbm, row0, stage_ref, sem, after_chunk=None):
    n_chunks = src_ref.shape[0] // LN_ROWS
    assert n_chunks * LN_ROWS == src_ref.shape[0] and n_chunks % N_STAGE == 0

    def copy(c, slot):
        return pltpu.make_async_copy(stage_ref.at[slot], dst_hbm.at[_chunk_rows(c, row0), :], sem.at[slot])

    def body(c2, carry):
        for slot in range(N_STAGE):
            c = c2 * N_STAGE + slot

            @pl.when(c2 > 0)
            def _():
                copy(c - N_STAGE, slot).wait()

            stage_ref[slot] = _layer_norm_rows(src_ref[_chunk_rows(c), :], g_ref[...], b_ref[...])
            copy(c, slot).start()
            if after_chunk is not None:
                after_chunk(c)
        return carry

    lax.fori_loop(0, n_chunks // N_STAGE, body, 0)
    for slot in range(N_STAGE):
        copy(n_chunks - N_STAGE + slot, slot).wait()


def _stage_scratch():
    return [pltpu.VMEM((N_STAGE, LN_ROWS, D_MODEL), F32), pltpu.SemaphoreType.DMA((N_STAGE,))]


def _out_proj_kernel(att_ref, pool_ref, x_ref, w_ref, o_ref, *maybe_wb_ref):
    if maybe_wb_ref:
        wb_ref, = maybe_wb_ref
        wb_ref[...] = w_ref[...].astype(BF16)
    else:
        wb_ref = w_ref
    acc = jnp.dot(att_ref[...], wb_ref[:D_ATT, :], preferred_element_type=F32)
    acc = acc + jnp.dot(pool_ref[...], wb_ref[D_ATT:, :], preferred_element_type=F32)
    o_ref[...] = DEEPNORM_ALPHA * x_ref[...] + acc


def _out_proj(att_o, pool_o, x, w, tm, tn):
    m = x.shape[0]
    n_blocks = D_MODEL // tn
    cast_weights = w.dtype == F32
    blocked = pl.BlockSpec((None, D_MODEL, tn), lambda i, j: (j, 0, 0))
    out_specs = [pl.BlockSpec((tm, tn), lambda i, j: (i, j))]
    out_shape = [jax.ShapeDtypeStruct((m, D_MODEL), F32)]
    if cast_weights:
        assert m == tm
        w_spec = pl.BlockSpec((D_MODEL, tn), lambda i, j: (0, j))
        out_specs.append(blocked)
        out_shape.append(jax.ShapeDtypeStruct((n_blocks, D_MODEL, tn), BF16))
    else:
        w_spec = blocked
    return pl.pallas_call(
        _out_proj_kernel,
        grid=(m // tm, n_blocks),
        in_specs=[
            pl.BlockSpec((tm, D_ATT), lambda i, j: (i, 0)),
            pl.BlockSpec((tm, D_POOL), lambda i, j: (i, 0)),
            pl.BlockSpec((tm, tn), lambda i, j: (i, j)),
            w_spec,
        ],
        out_specs=out_specs,
        out_shape=out_shape,
        compiler_params=_params(("arbitrary", "arbitrary")),
        name="out_proj",
    )(att_o, pool_o, x, w)


DOWN_COLS = 512
SPLIT_LAG = 1


def _ffn_kernel(h_hbm, hx_hbm, wg_ref, wu_ref, wd_ref, g1_ref, b1_ref, g_ref, b_ref, y_hbm, yx_hbm,
                acc_ref, xb_ref, in_sem, stage_ref, out_sem, *, n_blocks, tm, tx, n_tiles):
    i = pl.program_id(0)
    j = pl.program_id(1)
    n_chunks = tm // LN_ROWS
    x_chunks = tx // LN_ROWS

    def x_copy(tile, c):
        return pltpu.make_async_copy(h_hbm.at[_chunk_rows(c, tile * tm), :], acc_ref.at[_chunk_rows(c), :],
                                     in_sem.at[c])

    def extra_copy(k):
        return pltpu.make_async_copy(hx_hbm.at[_chunk_rows(k), :], acc_ref.at[_chunk_rows(n_chunks + k), :],
                                     in_sem.at[n_chunks + k])

    def split(copy, c):
        copy.wait()
        x = _layer_norm_rows(acc_ref[_chunk_rows(c), :], g1_ref[...], b1_ref[...])
        xb_ref[_chunk_rows(c), :] = x.astype(BF16)
        acc_ref[_chunk_rows(c), :] = DEEPNORM_ALPHA * x

    @pl.when(j == 0)
    def _():
        @pl.when(i == 0)
        def _():
            for c in range(n_chunks):
                x_copy(0, c).start()
            for k in range(x_chunks):
                extra_copy(k).start()

        def split_rest(c, carry):
            split(x_copy(i, c), c)
            return carry

        lax.fori_loop(jnp.where(i == 0, 0, n_chunks - SPLIT_LAG), n_chunks, split_rest, 0)

        @pl.when(i == 0)
        def _():
            for k in range(x_chunks):
                split(extra_copy(k), n_chunks + k)

    def block_step(rows):
        xb = xb_ref[:rows, :]
        gate = jnp.dot(xb, wg_ref[...], preferred_element_type=F32)
        up = jnp.dot(xb, wu_ref[...], preferred_element_type=F32)
        act = (gate * (1.0 / (1.0 + jnp.exp(-gate))) * up).astype(BF16)
        for n in range(0, D_MODEL, DOWN_COLS):
            cols = slice(n, n + DOWN_COLS)
            acc_ref[:rows, cols] += jnp.dot(act, wd_ref[:, cols], preferred_element_type=F32)

    @pl.when(i == 0)
    def _():
        block_step(tm + tx)

    @pl.when(i > 0)
    def _():
        block_step(tm)

    @pl.when(j == n_blocks - 1)
    def _():
        def prefetch(c):
            @pl.when(i + 1 < n_tiles)
            def _():
                x_copy(i + 1, c).start()

                @pl.when(c >= SPLIT_LAG)
                def _():
                    split(x_copy(i + 1, c - SPLIT_LAG), c - SPLIT_LAG)

        _layer_norm_to_hbm(acc_ref.at[pl.ds(0, tm), :], g_ref, b_ref, y_hbm, i * tm, stage_ref, out_sem,
                           after_chunk=prefetch)

        @pl.when(i == 0)
        def _():
            _layer_norm_to_hbm(acc_ref.at[pl.ds(tm, tx), :], g_ref, b_ref, yx_hbm, 0, stage_ref, out_sem)


def _ffn_ln(h, hx, wg_blocks, wu_blocks, wd_b, g1, b1, g2, b2, tm):
    m, tx = h.shape[0], hx.shape[0]
    n_blocks, _, tf = wg_blocks.shape
    n_tiles = m // tm
    n_chunks, x_chunks = tm // LN_ROWS, tx // LN_ROWS
    assert n_tiles * tm == m and n_chunks * LN_ROWS == tm and x_chunks * LN_ROWS == tx
    assert SPLIT_LAG <= n_chunks
    return pl.pallas_call(
        functools.partial(_ffn_kernel, n_blocks=n_blocks, tm=tm, tx=tx, n_tiles=n_tiles),
        grid=(n_tiles, n_blocks),
        in_specs=[
            pl.BlockSpec(memory_space=pl.ANY),
            pl.BlockSpec(memory_space=pl.ANY),
            pl.BlockSpec((None, D_MODEL, tf), lambda i, j: (j, 0, 0)),
            pl.BlockSpec((None, D_MODEL, tf), lambda i, j: (j, 0, 0)),
            pl.BlockSpec((tf, D_MODEL), lambda i, j: (j, 0)),
        ] + [pl.BlockSpec((1, D_MODEL), lambda i, j: (0, 0))] * 4,
        out_specs=[pl.BlockSpec(memory_space=pl.ANY), pl.BlockSpec(memory_space=pl.ANY)],
        out_shape=[jax.ShapeDtypeStruct((m, D_MODEL), F32), jax.ShapeDtypeStruct((tx, D_MODEL), F32)],
        scratch_shapes=[pltpu.VMEM((tm + tx, D_MODEL), F32), pltpu.VMEM((tm + tx, D_MODEL), BF16),
                        pltpu.SemaphoreType.DMA((n_chunks + x_chunks,))] + _stage_scratch(),
        compiler_params=_params(("arbitrary", "arbitrary")),
        name="ffn_ln",
    )(h, hx, wg_blocks, wu_blocks, wd_b, g1, b1, g2, b2)


IN_PROJ_COLS = 512
OUT_PROJ_COLS = 1024
FFN_COLS = 256
ROW_TILE = 1024
W_DOWN_CAST_ROWS = 256
POOL_ROWS = 1024


def kernel(x_prompt, x_sample, cache_k, cache_v, state_pool, w_in, rel_bias, w_pool, pool_scale,
           w_out, ln1_g, ln1_b, w_gate, w_up, w_down, ln2_g, ln2_b):
    batch, seq, _ = x_prompt.shape
    dec_batch, dec_seq, _ = x_sample.shape
    depth = w_in.shape[0]
    assert depth == 1 and seq % GROUP_ROWS == 0 and seq >= ATT_WINDOW
    lc = cache_k.shape[2]
    n_dec = dec_batch * dec_seq

    n_groups = len(POOL_WINDOWS)
    w_pool_b = _cast_rows(w_pool[0].reshape(n_groups * POOL_GROUP_W, POOL_GROUP_W),
                          n_groups * POOL_GROUP_W).reshape(n_groups, POOL_GROUP_W, POOL_GROUP_W)
    bias_rows = _bias_rows(rel_bias[0])
    g1, b1 = ln1_g[0][None], ln1_b[0][None]
    g2, b2 = ln2_g[0][None], ln2_b[0][None]
    scale = pool_scale[0][None]
    xp = x_prompt.reshape(batch * seq, D_MODEL)
    xs = x_sample.reshape(n_dec, D_MODEL)

    hs, w_in_b = _in_proj_cast(xs, w_in[0], IN_PROJ_COLS)
    hp, w_gate_b, w_up_b = _in_proj(xp, w_in_b, w_gate[0], w_up[0], ROW_TILE, FFN_COLS)

    att_s = _sample_attention(hs, cache_k[0].reshape(dec_batch, lc * N_HEADS, HEAD_DIM),
                              cache_v[0].reshape(dec_batch, lc * N_HEADS, HEAD_DIM),
                              bias_rows, dec_batch, dec_seq)
    pool_s, ps_new = _sample_pool(hs, state_pool[0], w_pool_b, scale, dec_batch, dec_seq)
    res_s, w_out_b = _out_proj(att_s, pool_s, xs, w_out[0], n_dec, OUT_PROJ_COLS)

    att_p, w_down_b, kp_new, vp_new = _prompt_attention(hp, bias_rows, w_down[0], batch, seq,
                                                        W_DOWN_CAST_ROWS)
    pool_p = _prompt_pool(hp, w_pool_b, scale, batch, seq, POOL_ROWS)
    res_p, = _out_proj(att_p, pool_p, xp, w_out_b, ROW_TILE, OUT_PROJ_COLS)

    yp, ys = _ffn_ln(res_p, res_s, w_gate_b, w_up_b, w_down_b, g1, b1, g2, b2, ROW_TILE)

    hp3 = hp.reshape(batch, seq, 4 * D_ATT)
    dec_shape = (1, dec_batch, dec_seq, N_HEADS, HEAD_DIM)
    return (yp.reshape(batch, seq, D_MODEL),
            ys.reshape(dec_batch, dec_seq, D_MODEL),
            kp_new,
            vp_new,
            hp3[None, :, seq - POOL_HIST:, 3 * D_ATT:],
            hs[:, D_ATT:2 * D_ATT].reshape(dec_shape),
            hs[:, 2 * D_ATT:3 * D_ATT].reshape(dec_shape),
            ps_new[None])
```

```python
import functools

import jax
import jax.numpy as jnp
from jax import lax
from jax.experimental import pallas as pl
from jax.experimental.pallas import tpu as pltpu

D_MODEL = 4096
CHUNK = 64
N_PREV_CHUNKS = 8
ATT_WINDOW = N_PREV_CHUNKS * CHUNK
D_ATT = D_MODEL // 2
D_POOL = D_MODEL - D_ATT
N_HEADS = 16
HEAD_DIM = D_ATT // N_HEADS
REL_CLIP = 256
POOL_WINDOWS = (2, 4, 8, 16)
POOL_GROUP_W = D_POOL // len(POOL_WINDOWS)
POOL_HIST = max(POOL_WINDOWS) - 1
LN_EPS = 1e-5
DEEPNORM_ALPHA = 2.0 ** 0.25
NEG_INF = -1e30
ATT_SCALE = HEAD_DIM ** -0.5
LOG2_E = 1.4426950408889634
Q_SCALE = ATT_SCALE * LOG2_E

GROUP_CHUNKS = 4
GROUP_ROWS = GROUP_CHUNKS * CHUNK
BAND_COLS = GROUP_ROWS + ATT_WINDOW
BIAS_PERIOD = 1024

VMEM_LIMIT_BYTES = 56 * 1024 * 1024

F32 = jnp.float32
BF16 = jnp.bfloat16


def _params(semantics):
    return pltpu.CompilerParams(dimension_semantics=semantics, vmem_limit_bytes=VMEM_LIMIT_BYTES)


def _cast_kernel(w_ref, o_ref):
    o_ref[...] = w_ref[...].astype(BF16)


def _cast_rows(w, rows):
    r, c = w.shape
    return pl.pallas_call(
        _cast_kernel,
        grid=(r // rows,),
        in_specs=[pl.BlockSpec((rows, c), lambda i: (i, 0))],
        out_specs=pl.BlockSpec((rows, c), lambda i: (i, 0)),
        out_shape=jax.ShapeDtypeStruct((r, c), BF16),
        compiler_params=_params(("arbitrary",)),
        name="cast_rows",
    )(w)


def _cast_to_col_blocks(w_ref, o_ref):
    nb, _, tf = o_ref.shape
    for c in range(nb):
        o_ref[c] = w_ref[:, c * tf:(c + 1) * tf].astype(BF16)


def _project(xb_ref, w, o_ref, j, q_blocks):
    acc = jnp.dot(xb_ref[...], w, preferred_element_type=F32)
    o_ref[...] = acc * jnp.where(j < q_blocks, Q_SCALE, 1.0)


def _in_proj_cast_kernel(x_ref, w_ref, o_ref, wb_ref, xb_ref, *, q_blocks):
    j = pl.program_id(0)

    @pl.when(j == 0)
    def _():
        xb_ref[...] = x_ref[...].astype(BF16)

    wb_ref[...] = w_ref[...].astype(BF16)
    _project(xb_ref, wb_ref[...], o_ref, j, q_blocks)


def _in_proj_cast(x, w, tn):
    m = x.shape[0]
    n = w.shape[1]
    return pl.pallas_call(
        functools.partial(_in_proj_cast_kernel, q_blocks=D_ATT // tn),
        grid=(n // tn,),
        in_specs=[
            pl.BlockSpec((m, D_MODEL), lambda j: (0, 0), pipeline_mode=pl.Buffered(1)),
            pl.BlockSpec((D_MODEL, tn), lambda j: (0, j)),
        ],
        out_specs=[pl.BlockSpec((m, tn), lambda j: (0, j)),
                   pl.BlockSpec((None, D_MODEL, tn), lambda j: (j, 0, 0))],
        out_shape=[jax.ShapeDtypeStruct((m, n), F32),
                   jax.ShapeDtypeStruct((n // tn, D_MODEL, tn), BF16)],
        scratch_shapes=[pltpu.VMEM((m, D_MODEL), BF16)],
        compiler_params=_params(("arbitrary",)),
        name="in_proj_cast",
    )(x, w)


X_CHUNK = 128
X_SLOTS = 2


def _in_proj_kernel(x_hbm, w_ref, wg_ref, wu_ref, o_ref, wgb_ref, wub_ref, xb_ref, xs_ref, x_sem,
                    *, q_blocks, tm, n_tiles):
    i = pl.program_id(0)
    j = pl.program_id(1)
    n_chunks = tm // X_CHUNK

    def fetch(tile, c):
        src = x_hbm.at[pl.ds(pl.multiple_of(tile * tm + c * X_CHUNK, X_CHUNK), X_CHUNK), :]
        return pltpu.make_async_copy(src, xs_ref.at[c % X_SLOTS], x_sem.at[c % X_SLOTS])

    def convert(tile, c):
        fetch(tile, c).wait()
        rows = pl.ds(pl.multiple_of(c * X_CHUNK, X_CHUNK), X_CHUNK)
        xb_ref[tile % 2, rows, :] = xs_ref[c % X_SLOTS].astype(BF16)

    @pl.when((i == 0) & (j == 0))
    def _():
        for c in range(X_SLOTS):
            fetch(0, c).start()

        def first_tile(c, carry):
            convert(0, c)

            @pl.when(c + X_SLOTS < n_chunks)
            def _():
                fetch(0, c + X_SLOTS).start()
            return carry

        lax.fori_loop(0, n_chunks, first_tile, 0)

    @pl.when(i + 1 < n_tiles)
    def _():
        @pl.when((j >= 1) & (j <= n_chunks))
        def _():
            convert(i + 1, j - 1)

        @pl.when(j < n_chunks)
        def _():
            fetch(i + 1, j).start()

    _project(xb_ref.at[i % 2], w_ref[...], o_ref, j, q_blocks)
    _cast_to_col_blocks(wg_ref, wgb_ref)
    _cast_to_col_blocks(wu_ref, wub_ref)


def _in_proj(x, w_blocks, w_gate, w_up, tm, tf):
    m = x.shape[0]
    nb, _, tn = w_blocks.shape
    d_ff = w_gate.shape[1]
    n_tiles = m // tm
    steps = n_tiles * nb
    slab = D_MODEL // steps
    assert slab * steps == D_MODEL and slab % 16 == 0 and d_ff % tf == 0
    n_chunks = tm // X_CHUNK
    assert n_tiles * tm == m and n_chunks * X_CHUNK == tm and X_SLOTS <= n_chunks < nb
    side_in = lambda: pl.BlockSpec((slab, d_ff), lambda i, j: (i * nb + j, 0))
    side_out = lambda: pl.BlockSpec((d_ff // tf, slab, tf), lambda i, j: (0, i * nb + j, 0))
    side_shape = jax.ShapeDtypeStruct((d_ff // tf, D_MODEL, tf), BF16)
    return pl.pallas_call(
        functools.partial(_in_proj_kernel, q_blocks=D_ATT // tn, tm=tm, n_tiles=n_tiles),
        grid=(n_tiles, nb),
        in_specs=[
            pl.BlockSpec(memory_space=pl.ANY),
            pl.BlockSpec((None, D_MODEL, tn), lambda i, j: (j, 0, 0)),
            side_in(), side_in(),
        ],
        out_specs=[pl.BlockSpec((tm, tn), lambda i, j: (i, j)), side_out(), side_out()],
        out_shape=[jax.ShapeDtypeStruct((m, nb * tn), F32), side_shape, side_shape],
        scratch_shapes=[pltpu.VMEM((2, tm, D_MODEL), BF16), pltpu.VMEM((X_SLOTS, X_CHUNK, D_MODEL), F32),
                        pltpu.SemaphoreType.DMA((X_SLOTS,))],
        compiler_params=_params(("arbitrary", "arbitrary")),
        name="in_proj",
    )(x, w_blocks, w_gate, w_up)


def _bias_rows(table):
    far = table[:, 2 * REL_CLIP:]
    head = jnp.broadcast_to(far, (table.shape[0], ATT_WINDOW - REL_CLIP))
    tail = jnp.broadcast_to(far, (table.shape[0], BIAS_PERIOD - (ATT_WINDOW + REL_CLIP + 1)))
    return (LOG2_E * jnp.concatenate([head, table[:, ::-1], tail], axis=1))[:, None, :]


def _toeplitz_bias(r, rows):
    return pltpu.roll(jnp.broadcast_to(r, (rows, BIAS_PERIOD)), 0, 1, stride=1, stride_axis=0)


def _prompt_attn_kernel(r_ref, q_ref, k_ref, v_ref, wd_ref, o_ref, wdb_ref, kn_hbm, vn_hbm,
                        bias_ref, qb_ref, kb_ref, vb_ref, new_sem, *, seq, wd_blocks):
    keep = kn_hbm.shape[2]
    hd, b = pl.program_id(0), pl.program_id(1)
    new_copies = [
        pltpu.make_async_copy(src.at[pl.ds(seq - keep, keep), :], dst.at[0, b, :, hd, :], new_sem.at[n])
        for n, (src, dst) in enumerate(((k_ref, kn_hbm), (v_ref, vn_hbm)))]
    for cp in new_copies:
        cp.start()

    @pl.when(hd * pl.num_programs(1) + b < wd_blocks)
    def _():
        wdb_ref[...] = wd_ref[...].astype(BF16)

    @pl.when(b == 0)
    def _():
        t = _toeplitz_bias(r_ref[...], GROUP_ROWS)[:, :BAND_COLS]
        i = lax.broadcasted_iota(jnp.int32, (GROUP_ROWS, BAND_COLS), 0)
        j = lax.broadcasted_iota(jnp.int32, (GROUP_ROWS, BAND_COLS), 1)
        gap = j // CHUNK - i // CHUNK
        bias_ref[...] = jnp.where((gap >= 0) & (gap <= N_PREV_CHUNKS), t, NEG_INF)

    @pl.when((hd == 0) & (b == 0))
    def _():
        lane = lax.broadcasted_iota(jnp.int32, (seq, HEAD_DIM), 1)
        vb_ref[:, HEAD_DIM:] = jnp.where(lane == 0, 1.0, 0.0).astype(BF16)

    qb_ref[...] = q_ref[...].astype(BF16)
    kb_ref[...] = k_ref[...].astype(BF16)
    vb_ref[:, :HEAD_DIM] = v_ref[...].astype(BF16)
    for g in range(seq // GROUP_ROWS):
        q0 = g * GROUP_ROWS
        k0 = max(q0 - ATT_WINDOW, 0)
        nk = q0 + GROUP_ROWS - k0
        c0 = BAND_COLS - nk
        s = lax.dot_general(qb_ref[pl.ds(q0, GROUP_ROWS), :], kb_ref[pl.ds(k0, nk), :],
                            (((1,), (1,)), ((), ())), preferred_element_type=F32)
        s = s + bias_ref[:, c0:]
        m = jnp.max(s, axis=-1, keepdims=True)
        p = jnp.exp2(s - m).astype(BF16)
        o = jnp.dot(p, vb_ref[pl.ds(k0, nk), :], preferred_element_type=F32)
        o_ref[pl.ds(q0, GROUP_ROWS), :] = (o[:, :HEAD_DIM] / o[:, HEAD_DIM:HEAD_DIM + 1]).astype(BF16)
    for cp in new_copies:
        cp.wait()


def _prompt_attention(h, bias_rows, w_down, batch, seq, wd_rows):
    keep = min(ATT_WINDOW, seq)
    new_rows = lambda: pl.BlockSpec(memory_space=pl.ANY)
    new_shape = jax.ShapeDtypeStruct((1, batch, keep, N_HEADS, HEAD_DIM), F32)
    d_ff = w_down.shape[0]
    wd_blocks = d_ff // wd_rows
    assert wd_blocks * wd_rows == d_ff and wd_blocks <= N_HEADS * batch

    def col(seg):
        return pl.BlockSpec((seq, HEAD_DIM), lambda hd, b: (b, seg * N_HEADS + hd))

    wd_spec = lambda: pl.BlockSpec((wd_rows, D_MODEL),
                                   lambda hd, b: (jnp.minimum(hd * batch + b, wd_blocks - 1), 0))
    return pl.pallas_call(
        functools.partial(_prompt_attn_kernel, seq=seq, wd_blocks=wd_blocks),
        grid=(N_HEADS, batch),
        in_specs=[pl.BlockSpec((None, 1, BIAS_PERIOD), lambda hd, b: (hd, 0, 0)), col(0), col(1), col(2),
                  wd_spec()],
        out_specs=[pl.BlockSpec((seq, HEAD_DIM), lambda hd, b: (b, hd)), wd_spec(), new_rows(), new_rows()],
        out_shape=[jax.ShapeDtypeStruct((batch * seq, D_ATT), BF16),
                   jax.ShapeDtypeStruct((d_ff, D_MODEL), BF16), new_shape, new_shape],
        scratch_shapes=[pltpu.VMEM((GROUP_ROWS, BAND_COLS), F32)]
        + [pltpu.VMEM((seq, HEAD_DIM), BF16)] * 2 + [pltpu.VMEM((seq, 2 * HEAD_DIM), BF16)]
        + [pltpu.SemaphoreType.DMA((2,))],
        compiler_params=_params(("arbitrary", "arbitrary")),
        name="prompt_attention",
    )(bias_rows, h, h, h, w_down)


def _sample_attn_kernel(r_ref, q_ref, kn_ref, vn_ref, ck_ref, cv_ref, o_ref, bias_ref, kt_ref, vt_ref, *, lc, t):
    @pl.when(pl.program_id(0) == 0)
    def _():
        for h in range(N_HEADS):
            bias_ref[h] = _toeplitz_bias(r_ref[h], t)

    for src, dst in ((ck_ref, kt_ref), (cv_ref, vt_ref)):
        dst[...] = jnp.swapaxes(src[...].reshape(lc, N_HEADS, HEAD_DIM), 0, 1).astype(BF16)

    contract_last = (((1,), (1,)), ((), ()))
    for h in range(N_HEADS):
        cols = slice(h * HEAD_DIM, (h + 1) * HEAD_DIM)
        q = q_ref[:, cols].astype(BF16)
        kc = kt_ref[h]
        vc = vt_ref[h]
        kn = kn_ref[:, cols].astype(BF16)
        vn = vn_ref[:, cols].astype(BF16)
        s_c = lax.dot_general(q, kc, contract_last, preferred_element_type=F32) + bias_ref[h, :, :lc]
        s_n = lax.dot_general(q, kn, contract_last, preferred_element_type=F32) + bias_ref[h, :, lc:lc + t]
        m = jnp.maximum(jnp.max(s_c, axis=-1, keepdims=True), jnp.max(s_n, axis=-1, keepdims=True))
        p_c = jnp.exp2(s_c - m)
        p_n = jnp.exp2(s_n - m)
        l = jnp.sum(p_c, axis=-1, keepdims=True) + jnp.sum(p_n, axis=-1, keepdims=True)
        o = (jnp.dot(p_c.astype(BF16), vc, preferred_element_type=F32)
             + jnp.dot(p_n.astype(BF16), vn, preferred_element_type=F32))
        o_ref[:, cols] = (o / l).astype(BF16)


def _sample_attention(h, cache_k, cache_v, bias_rows, batch, t):
    lc = cache_k.shape[1] // N_HEADS
    assert lc == ATT_WINDOW and t <= CHUNK
    row = lambda seg: pl.BlockSpec((t, D_ATT), lambda b: (b, seg))
    cache = lambda: pl.BlockSpec((None, lc * N_HEADS, HEAD_DIM), lambda b: (b, 0, 0))
    return pl.pallas_call(
        functools.partial(_sample_attn_kernel, lc=lc, t=t),
        grid=(batch,),
        in_specs=[pl.BlockSpec((N_HEADS, 1, BIAS_PERIOD), lambda b: (0, 0, 0)),
                  row(0), row(1), row(2), cache(), cache()],
        out_specs=pl.BlockSpec((t, D_ATT), lambda b: (b, 0)),
        out_shape=jax.ShapeDtypeStruct((batch * t, D_ATT), BF16),
        scratch_shapes=[pltpu.VMEM((N_HEADS, t, BIAS_PERIOD), F32)]
        + [pltpu.VMEM((N_HEADS, lc, HEAD_DIM), BF16)] * 2,
        compiler_params=_params(("arbitrary",)),
        name="sample_attention",
    )(bias_rows, h, h, h, cache_k, cache_v)


HALO = POOL_HIST + 1
U_SEG = 3 * D_ATT // D_POOL
POOL_STAGES = max(POOL_WINDOWS).bit_length() - 1
POOL_HALO = 8 * POOL_STAGES


def _prompt_pool_kernel(u_ref, halo_ref, wp_ref, scale_ref, o_ref, full_ref, ping_ref, pong_ref, *, tm):
    t = pl.program_id(1)

    @pl.when(t == 0)
    def _():
        full_ref[0:POOL_HALO, :] = jnp.zeros((POOL_HALO, D_POOL), F32)

    @pl.when(t > 0)
    def _():
        full_ref[0:POOL_HALO, :] = halo_ref[...]

    full_ref[POOL_HALO:, :] = u_ref[...]
    rows = POOL_HALO + tm
    pos = t * tm + lax.broadcasted_iota(jnp.int32, (tm, 1), 0)
    for g, w in enumerate(POOL_WINDOWS):
        cols = slice(g * POOL_GROUP_W, (g + 1) * POOL_GROUP_W)
        src, src_cols = full_ref, cols
        for k in range(1, w.bit_length()):
            lo, back = 8 * k, 2 ** (k - 1)
            dst = (ping_ref, pong_ref)[k % 2]
            dst[lo:, :] = src[lo:, src_cols] + src[lo - back:rows - back, src_cols]
            src, src_cols = dst, slice(None)
        wsum = src[POOL_HALO:, src_cols]
        cnt = jnp.minimum(w, pos + 1).astype(F32)
        d = wsum / cnt - full_ref[POOL_HALO:, cols]
        y = jnp.dot(d.astype(BF16), wp_ref[g], preferred_element_type=F32)
        o_ref[:, cols] = (y * scale_ref[:, cols]).astype(BF16)


def _prompt_pool(h, w_pool_b, pool_scale, batch, seq, tm):
    tiles = seq // tm
    halo_per_tile = tm // POOL_HALO
    assert halo_per_tile * POOL_HALO == tm and POOL_HALO >= POOL_HIST
    stage = pltpu.VMEM((POOL_HALO + tm, POOL_GROUP_W), F32)
    return pl.pallas_call(
        functools.partial(_prompt_pool_kernel, tm=tm),
        grid=(batch, tiles),
        in_specs=[
            pl.BlockSpec((tm, D_POOL), lambda b, t: (b * tiles + t, U_SEG)),
            pl.BlockSpec((POOL_HALO, D_POOL),
                         lambda b, t: (jnp.maximum((b * tiles + t) * halo_per_tile - 1, 0), U_SEG)),
            pl.BlockSpec((len(POOL_WINDOWS), POOL_GROUP_W, POOL_GROUP_W), lambda b, t: (0, 0, 0)),
            pl.BlockSpec((1, D_POOL), lambda b, t: (0, 0)),
        ],
        out_specs=pl.BlockSpec((tm, D_POOL), lambda b, t: (b * tiles + t, 0)),
        out_shape=jax.ShapeDtypeStruct((batch * seq, D_POOL), BF16),
        scratch_shapes=[pltpu.VMEM((POOL_HALO + tm, D_POOL), F32), stage, stage],
        compiler_params=_params(("arbitrary", "arbitrary")),
        name="prompt_pool",
    )(h, h, w_pool_b, pool_scale)


def _sample_pool_kernel(u_ref, state_ref, wp_ref, scale_ref, o_ref, new_state_ref, full_ref, *, batch, t):
    full_ref[:, 1:HALO, :] = state_ref[...]
    full_ref[:, HALO:, :] = u_ref[...].reshape(batch, t, D_POOL)
    for g, w in enumerate(POOL_WINDOWS):
        cols = slice(g * POOL_GROUP_W, (g + 1) * POOL_GROUP_W)
        wsum = full_ref[:, HALO:, cols]
        for s in range(1, w):
            wsum = wsum + full_ref[:, HALO - s:HALO - s + t, cols]
        d = wsum / float(w) - full_ref[:, HALO:, cols]
        y = jnp.dot(d.reshape(batch * t, POOL_GROUP_W).astype(BF16), wp_ref[g],
                    preferred_element_type=F32)
        o_ref[:, cols] = (y * scale_ref[:, cols]).astype(BF16)
    new_state_ref[...] = full_ref[:, HALO + t - POOL_HIST:, :]


def _sample_pool(h, state, w_pool_b, pool_scale, batch, t):
    full = lambda shape: pl.BlockSpec(shape, lambda i: (0,) * len(shape))
    return pl.pallas_call(
        functools.partial(_sample_pool_kernel, batch=batch, t=t),
        grid=(1,),
        in_specs=[pl.BlockSpec((batch * t, D_POOL), lambda i: (0, U_SEG)),
                  full((batch, POOL_HIST, D_POOL)),
                  full((len(POOL_WINDOWS), POOL_GROUP_W, POOL_GROUP_W)), full((1, D_POOL))],
        out_specs=[full((batch * t, D_POOL)), full((batch, POOL_HIST, D_POOL))],
        out_shape=[jax.ShapeDtypeStruct((batch * t, D_POOL), BF16),
                   jax.ShapeDtypeStruct((batch, POOL_HIST, D_POOL), F32)],
        scratch_shapes=[pltpu.VMEM((batch, HALO + t, D_POOL), F32)],
        compiler_params=_params(("arbitrary",)),
        name="sample_pool",
    )(h, state, w_pool_b, pool_scale)


LN_ROWS = 128


def _layer_norm_rows(h, g, b):
    mu = jnp.mean(h, axis=-1, keepdims=True)
    c = h - mu
    var = jnp.mean(c * c, axis=-1, keepdims=True)
    return c * lax.rsqrt(var + LN_EPS) * g + b


N_STAGE = 2


def _chunk_rows(c, base=0):
    return pl.ds(pl.multiple_of(base + c * LN_ROWS, LN_ROWS), LN_ROWS)


def _layer_norm_to_hbm(src_ref, g_ref, b_ref, dst_hbm, row0, stage_ref, sem, after_chunk=None):
    n_chunks = src_ref.shape[0] // LN_ROWS
    assert n_chunks * LN_ROWS == src_ref.shape[0] and n_chunks % N_STAGE == 0

    def copy(c, slot):
        return pltpu.make_async_copy(stage_ref.at[slot], dst_hbm.at[_chunk_rows(c, row0), :], sem.at[slot])

    def body(c2, carry):
        for slot in range(N_STAGE):
            c = c2 * N_STAGE + slot

            @pl.when(c2 > 0)
            def _():
                copy(c - N_STAGE, slot).wait()

            stage_ref[slot] = _layer_norm_rows(src_ref[_chunk_rows(c), :], g_ref[...], b_ref[...])
            copy(c, slot).start()
            if after_chunk is not None:
                after_chunk(c)
        return carry

    lax.fori_loop(0, n_chunks // N_STAGE, body, 0)
    for slot in range(N_STAGE):
        copy(n_chunks - N_STAGE + slot, slot).wait()


def _stage_scratch():
    return [pltpu.VMEM((N_STAGE, LN_ROWS, D_MODEL), F32), pltpu.SemaphoreType.DMA((N_STAGE,))]


def _out_proj_kernel(att_ref, pool_ref, x_ref, w_ref, o_ref, *maybe_wb_ref):
    if maybe_wb_ref:
        wb_ref, = maybe_wb_ref
        wb_ref[...] = w_ref[...].astype(BF16)
    else:
        wb_ref = w_ref
    acc = jnp.dot(att_ref[...], wb_ref[:D_ATT, :], preferred_element_type=F32)
    acc = acc + jnp.dot(pool_ref[...], wb_ref[D_ATT:, :], preferred_element_type=F32)
    o_ref[...] = DEEPNORM_ALPHA * x_ref[...] + acc


def _out_proj(att_o, pool_o, x, w, tm, tn):
    m = x.shape[0]
    n_blocks = D_MODEL // tn
    cast_weights = w.dtype == F32
    blocked = pl.BlockSpec((None, D_MODEL, tn), lambda i, j: (j, 0, 0))
    out_specs = [pl.BlockSpec((tm, tn), lambda i, j: (i, j))]
    out_shape = [jax.ShapeDtypeStruct((m, D_MODEL), F32)]
    if cast_weights:
        assert m == tm
        w_spec = pl.BlockSpec((D_MODEL, tn), lambda i, j: (0, j))
        out_specs.append(blocked)
        out_shape.append(jax.ShapeDtypeStruct((n_blocks, D_MODEL, tn), BF16))
    else:
        w_spec = blocked
    return pl.pallas_call(
        _out_proj_kernel,
        grid=(m // tm, n_blocks),
        in_specs=[
            pl.BlockSpec((tm, D_ATT), lambda i, j: (i, 0)),
            pl.BlockSpec((tm, D_POOL), lambda i, j: (i, 0)),
            pl.BlockSpec((tm, tn), lambda i, j: (i, j)),
            w_spec,
        ],
        out_specs=out_specs,
        out_shape=out_shape,
        compiler_params=_params(("arbitrary", "arbitrary")),
        name="out_proj",
    )(att_o, pool_o, x, w)


DOWN_COLS = 512
SPLIT_LAG = 1


def _ffn_kernel(h_hbm, hx_hbm, wg_ref, wu_ref, wd_ref, g1_ref, b1_ref, g_ref, b_ref, y_hbm, yx_hbm,
                acc_ref, xb_ref, in_sem, stage_ref, out_sem, *, n_blocks, tm, tx, n_tiles):
    i = pl.program_id(0)
    j = pl.program_id(1)
    n_chunks = tm // LN_ROWS
    x_chunks = tx // LN_ROWS

    def x_copy(tile, c):
        return pltpu.make_async_copy(h_hbm.at[_chunk_rows(c, tile * tm), :], acc_ref.at[_chunk_rows(c), :],
                                     in_sem.at[c])

    def extra_copy(k):
        return pltpu.make_async_copy(hx_hbm.at[_chunk_rows(k), :], acc_ref.at[_chunk_rows(n_chunks + k), :],
                                     in_sem.at[n_chunks + k])

    def split(copy, c):
        copy.wait()
        x = _layer_norm_rows(acc_ref[_chunk_rows(c), :], g1_ref[...], b1_ref[...])
        xb_ref[_chunk_rows(c), :] = x.astype(BF16)
        acc_ref[_chunk_rows(c), :] = DEEPNORM_ALPHA * x

    @pl.when(j == 0)
    def _():
        @pl.when(i == 0)
        def _():
            for c in range(n_chunks):
                x_copy(0, c).start()
            for k in range(x_chunks):
                extra_copy(k).start()

        def split_rest(c, carry):
            split(x_copy(i, c), c)
            return carry

        lax.fori_loop(jnp.where(i == 0, 0, n_chunks - SPLIT_LAG), n_chunks, split_rest, 0)

        @pl.when(i == 0)
        def _():
            for k in range(x_chunks):
                split(extra_copy(k), n_chunks + k)

    def block_step(rows):
        xb = xb_ref[:rows, :]
        gate = jnp.dot(xb, wg_ref[...], preferred_element_type=F32)
        up = jnp.dot(xb, wu_ref[...], preferred_element_type=F32)
        act = (gate * (1.0 / (1.0 + jnp.exp(-gate))) * up).astype(BF16)
        for n in range(0, D_MODEL, DOWN_COLS):
            cols = slice(n, n + DOWN_COLS)
            acc_ref[:rows, cols] += jnp.dot(act, wd_ref[:, cols], preferred_element_type=F32)

    @pl.when(i == 0)
    def _():
        block_step(tm + tx)

    @pl.when(i > 0)
    def _():
        block_step(tm)

    @pl.when(j == n_blocks - 1)
    def _():
        def prefetch(c):
            @pl.when(i + 1 < n_tiles)
            def _():
                x_copy(i + 1, c).start()

                @pl.when(c >= SPLIT_LAG)
                def _():
                    split(x_copy(i + 1, c - SPLIT_LAG), c - SPLIT_LAG)

        _layer_norm_to_hbm(acc_ref.at[pl.ds(0, tm), :], g_ref, b_ref, y_hbm, i * tm, stage_ref, out_sem,
                           after_chunk=prefetch)

        @pl.when(i == 0)
        def _():
            _layer_norm_to_hbm(acc_ref.at[pl.ds(tm, tx), :], g_ref, b_ref, yx_hbm, 0, stage_ref, out_sem)


def _ffn_ln(h, hx, wg_blocks, wu_blocks, wd_b, g1, b1, g2, b2, tm):
    m, tx = h.shape[0], hx.shape[0]
    n_blocks, _, tf = wg_blocks.shape
    n_tiles = m // tm
    n_chunks, x_chunks = tm // LN_ROWS, tx // LN_ROWS
    assert n_tiles * tm == m and n_chunks * LN_ROWS == tm and x_chunks * LN_ROWS == tx
    assert SPLIT_LAG <= n_chunks
    return pl.pallas_call(
        functools.partial(_ffn_kernel, n_blocks=n_blocks, tm=tm, tx=tx, n_tiles=n_tiles),
        grid=(n_tiles, n_blocks),
        in_specs=[
            pl.BlockSpec(memory_space=pl.ANY),
            pl.BlockSpec(memory_space=pl.ANY),
            pl.BlockSpec((None, D_MODEL, tf), lambda i, j: (j, 0, 0)),
            pl.BlockSpec((None, D_MODEL, tf), lambda i, j: (j, 0, 0)),
            pl.BlockSpec((tf, D_MODEL), lambda i, j: (j, 0)),
        ] + [pl.BlockSpec((1, D_MODEL), lambda i, j: (0, 0))] * 4,
        out_specs=[pl.BlockSpec(memory_space=pl.ANY), pl.BlockSpec(memory_space=pl.ANY)],
        out_shape=[jax.ShapeDtypeStruct((m, D_MODEL), F32), jax.ShapeDtypeStruct((tx, D_MODEL), F32)],
        scratch_shapes=[pltpu.VMEM((tm + tx, D_MODEL), F32), pltpu.VMEM((tm + tx, D_MODEL), BF16),
                        pltpu.SemaphoreType.DMA((n_chunks + x_chunks,))] + _stage_scratch(),
        compiler_params=_params(("arbitrary", "arbitrary")),
        name="ffn_ln",
    )(h, hx, wg_blocks, wu_blocks, wd_b, g1, b1, g2, b2)


IN_PROJ_COLS = 512
OUT_PROJ_COLS = 1024
FFN_COLS = 256
ROW_TILE = 1024
W_DOWN_CAST_ROWS = 256
POOL_ROWS = 1024


def kernel(x_prompt, x_sample, cache_k, cache_v, state_pool, w_in, rel_bias, w_pool, pool_scale,
           w_out, ln1_g, ln1_b, w_gate, w_up, w_down, ln2_g, ln2_b):
    batch, seq, _ = x_prompt.shape
    dec_batch, dec_seq, _ = x_sample.shape
    depth = w_in.shape[0]
    assert depth == 1 and seq % GROUP_ROWS == 0 and seq >= ATT_WINDOW
    lc = cache_k.shape[2]
    n_dec = dec_batch * dec_seq

    n_groups = len(POOL_WINDOWS)
    w_pool_b = _cast_rows(w_pool[0].reshape(n_groups * POOL_GROUP_W, POOL_GROUP_W),
                          n_groups * POOL_GROUP_W).reshape(n_groups, POOL_GROUP_W, POOL_GROUP_W)
    bias_rows = _bias_rows(rel_bias[0])
    g1, b1 = ln1_g[0][None], ln1_b[0][None]
    g2, b2 = ln2_g[0][None], ln2_b[0][None]
    scale = pool_scale[0][None]
    xp = x_prompt.reshape(batch * seq, D_MODEL)
    xs = x_sample.reshape(n_dec, D_MODEL)

    hs, w_in_b = _in_proj_cast(xs, w_in[0], IN_PROJ_COLS)
    hp, w_gate_b, w_up_b = _in_proj(xp, w_in_b, w_gate[0], w_up[0], ROW_TILE, FFN_COLS)

    att_s = _sample_attention(hs, cache_k[0].reshape(dec_batch, lc * N_HEADS, HEAD_DIM),
                              cache_v[0].reshape(dec_batch, lc * N_HEADS, HEAD_DIM),
                              bias_rows, dec_batch, dec_seq)
    pool_s, ps_new = _sample_pool(hs, state_pool[0], w_pool_b, scale, dec_batch, dec_seq)
    res_s, w_out_b = _out_proj(att_s, pool_s, xs, w_out[0], n_dec, OUT_PROJ_COLS)

    att_p, w_down_b, kp_new, vp_new = _prompt_attention(hp, bias_rows, w_down[0], batch, seq,
                                                        W_DOWN_CAST_ROWS)
    pool_p = _prompt_pool(hp, w_pool_b, scale, batch, seq, POOL_ROWS)
    res_p, = _out_proj(att_p, pool_p, xp, w_out_b, ROW_TILE, OUT_PROJ_COLS)

    yp, ys = _ffn_ln(res_p, res_s, w_gate_b, w_up_b, w_down_b, g1, b1, g2, b2, ROW_TILE)

    hp3 = hp.reshape(batch, seq, 4 * D_ATT)
    dec_shape = (1, dec_batch, dec_seq, N_HEADS, HEAD_DIM)
    return (yp.reshape(batch, seq, D_MODEL),
            ys.reshape(dec_batch, dec_seq, D_MODEL),
            kp_new,
            vp_new,
            hp3[None, :, seq - POOL_HIST:, 3 * D_ATT:],
            hs[:, D_ATT:2 * D_ATT].reshape(dec_shape),
            hs[:, 2 * D_ATT:3 * D_ATT].reshape(dec_shape),
            ps_new[None])
```

```python
import functools

import jax
import jax.numpy as jnp
from jax import lax
from jax.experimental import pallas as pl
from jax.experimental.pallas import tpu as pltpu

D_MODEL = 4096
CHUNK = 64
N_PREV_CHUNKS = 8
ATT_WINDOW = N_PREV_CHUNKS * CHUNK
D_ATT = D_MODEL // 2
D_POOL = D_MODEL - D_ATT
N_HEADS = 16
HEAD_DIM = D_ATT // N_HEADS
REL_CLIP = 256
POOL_WINDOWS = (2, 4, 8, 16)
POOL_GROUP_W = D_POOL // len(POOL_WINDOWS)
POOL_HIST = max(POOL_WINDOWS) - 1
LN_EPS = 1e-5
DEEPNORM_ALPHA = 2.0 ** 0.25
NEG_INF = -1e30
ATT_SCALE = HEAD_DIM ** -0.5
LOG2_E = 1.4426950408889634
Q_SCALE = ATT_SCALE * LOG2_E

GROUP_CHUNKS = 4
GROUP_ROWS = GROUP_CHUNKS * CHUNK
BAND_COLS = GROUP_ROWS + ATT_WINDOW
BIAS_PERIOD = 1024

VMEM_LIMIT_BYTES = 56 * 1024 * 1024

F32 = jnp.float32
BF16 = jnp.bfloat16


def _params(semantics):
    return pltpu.CompilerParams(dimension_semantics=semantics, vmem_limit_bytes=VMEM_LIMIT_BYTES)


def _cast_kernel(w_ref, o_ref):
    o_ref[...] = w_ref[...].astype(BF16)


def _cast_rows(w, rows):
    r, c = w.shape
    return pl.pallas_call(
        _cast_kernel,
        grid=(r // rows,),
        in_specs=[pl.BlockSpec((rows, c), lambda i: (i, 0))],
        out_specs=pl.BlockSpec((rows, c), lambda i: (i, 0)),
        out_shape=jax.ShapeDtypeStruct((r, c), BF16),
        compiler_params=_params(("arbitrary",)),
        name="cast_rows",
    )(w)


def _cast_to_col_blocks(w_ref, o_ref, zero):
    nb, rows, tf = o_ref.shape
    z = jnp.tile(zero, (rows // 8, tf // 128))
    for c in range(nb):
        o_ref[c] = (w_ref[:, c * tf:(c + 1) * tf] + z).astype(BF16)


def _project(xb_ref, w, o_ref, j, q_blocks):
    acc = jnp.dot(xb_ref[...], w, preferred_element_type=F32)
    o_ref[...] = acc * jnp.where(j < q_blocks, Q_SCALE, 1.0)


def _in_proj_cast_kernel(x_ref, w_ref, o_ref, wb_ref, xb_ref, *, q_blocks):
    j = pl.program_id(0)

    @pl.when(j == 0)
    def _():
        xb_ref[...] = x_ref[...].astype(BF16)

    wb_ref[...] = w_ref[...].astype(BF16)
    _project(xb_ref, wb_ref[...], o_ref, j, q_blocks)


def _in_proj_cast(x, w, tn):
    m = x.shape[0]
    n = w.shape[1]
    return pl.pallas_call(
        functools.partial(_in_proj_cast_kernel, q_blocks=D_ATT // tn),
        grid=(n // tn,),
        in_specs=[
            pl.BlockSpec((m, D_MODEL), lambda j: (0, 0), pipeline_mode=pl.Buffered(1)),
            pl.BlockSpec((D_MODEL, tn), lambda j: (0, j)),
        ],
        out_specs=[pl.BlockSpec((m, tn), lambda j: (0, j)),
                   pl.BlockSpec((None, D_MODEL, tn), lambda j: (j, 0, 0))],
        out_shape=[jax.ShapeDtypeStruct((m, n), F32),
                   jax.ShapeDtypeStruct((n // tn, D_MODEL, tn), BF16)],
        scratch_shapes=[pltpu.VMEM((m, D_MODEL), BF16)],
        compiler_params=_params(("arbitrary",)),
        name="in_proj_cast",
    )(x, w)


X_CHUNK = 128
X_SLOTS = 2


def _in_proj_kernel(x_hbm, w_ref, wg_ref, wu_ref, o_ref, wgb_ref, wub_ref, xb_ref, xs_ref, x_sem,
                    *, q_blocks, tm, n_tiles):
    i = pl.program_id(0)
    j = pl.program_id(1)
    n_chunks = tm // X_CHUNK

    def fetch(tile, c):
        src = x_hbm.at[pl.ds(pl.multiple_of(tile * tm + c * X_CHUNK, X_CHUNK), X_CHUNK), :]
        return pltpu.make_async_copy(src, xs_ref.at[c % X_SLOTS], x_sem.at[c % X_SLOTS])

    def convert(tile, c):
        fetch(tile, c).wait()
        rows = pl.ds(pl.multiple_of(c * X_CHUNK, X_CHUNK), X_CHUNK)
        xb_ref[tile % 2, rows, :] = xs_ref[c % X_SLOTS].astype(BF16)

    @pl.when((i == 0) & (j == 0))
    def _():
        for c in range(X_SLOTS):
            fetch(0, c).start()

        def first_tile(c, carry):
            convert(0, c)

            @pl.when(c + X_SLOTS < n_chunks)
            def _():
                fetch(0, c + X_SLOTS).start()
            return carry

        lax.fori_loop(0, n_chunks, first_tile, 0)

    @pl.when(i + 1 < n_tiles)
    def _():
        @pl.when((j >= 1) & (j <= n_chunks))
        def _():
            convert(i + 1, j - 1)

        @pl.when(j < n_chunks)
        def _():
            fetch(i + 1, j).start()

    xb = xb_ref.at[i % 2]
    half = D_MODEL // 2
    scale = jnp.where(j < q_blocks, Q_SCALE, 1.0)
    acc = jnp.dot(xb[:, :half], w_ref[:half, :], preferred_element_type=F32)
    bits = pltpu.bitcast(acc[:8, :128], jnp.uint32)
    zero = pltpu.bitcast(lax.shift_right_logical(lax.shift_right_logical(bits, jnp.uint32(16)), jnp.uint32(16)), F32)
    _cast_to_col_blocks(wg_ref, wgb_ref, zero)
    _cast_to_col_blocks(wu_ref, wub_ref, zero)
    acc = acc + jnp.dot(xb[:, half:], w_ref[half:, :], preferred_element_type=F32)
    o_ref[...] = acc * scale


def _in_proj(x, w_blocks, w_gate, w_up, tm, tf):
    m = x.shape[0]
    nb, _, tn = w_blocks.shape
    d_ff = w_gate.shape[1]
    n_tiles = m // tm
    steps = n_tiles * nb
    slab = D_MODEL // steps
    assert slab * steps == D_MODEL and slab % 16 == 0 and d_ff % tf == 0
    n_chunks = tm // X_CHUNK
    assert n_tiles * tm == m and n_chunks * X_CHUNK == tm and X_SLOTS <= n_chunks < nb
    side_in = lambda: pl.BlockSpec((slab, d_ff), lambda i, j: (i * nb + j, 0))
    side_out = lambda: pl.BlockSpec((d_ff // tf, slab, tf), lambda i, j: (0, i * nb + j, 0))
    side_shape = jax.ShapeDtypeStruct((d_ff // tf, D_MODEL, tf), BF16)
    return pl.pallas_call(
        functools.partial(_in_proj_kernel, q_blocks=D_ATT // tn, tm=tm, n_tiles=n_tiles),
        grid=(n_tiles, nb),
        in_specs=[
            pl.BlockSpec(memory_space=pl.ANY),
            pl.BlockSpec((None, D_MODEL, tn), lambda i, j: (j, 0, 0)),
            side_in(), side_in(),
        ],
        out_specs=[pl.BlockSpec((tm, tn), lambda i, j: (i, j)), side_out(), side_out()],
        out_shape=[jax.ShapeDtypeStruct((m, nb * tn), F32), side_shape, side_shape],
        scratch_shapes=[pltpu.VMEM((2, tm, D_MODEL), BF16), pltpu.VMEM((X_SLOTS, X_CHUNK, D_MODEL), F32),
                        pltpu.SemaphoreType.DMA((X_SLOTS,))],
        compiler_params=_params(("arbitrary", "arbitrary")),
        name="in_proj",
    )(x, w_blocks, w_gate, w_up)


def _bias_rows(table):
    far = table[:, 2 * REL_CLIP:]
    head = jnp.broadcast_to(far, (table.shape[0], ATT_WINDOW - REL_CLIP))
    tail = jnp.broadcast_to(far, (table.shape[0], BIAS_PERIOD - (ATT_WINDOW + REL_CLIP + 1)))
    return (LOG2_E * jnp.concatenate([head, table[:, ::-1], tail], axis=1))[:, None, :]


def _toeplitz_bias(r, rows):
    return pltpu.roll(jnp.broadcast_to(r, (rows, BIAS_PERIOD)), 0, 1, stride=1, stride_axis=0)


def _prompt_attn_kernel(r_ref, q_ref, k_ref, v_ref, wd_ref, o_ref, wdb_ref, kn_hbm, vn_hbm,
                        bias_ref, qb_ref, kb_ref, vb_ref, new_sem, *, seq, wd_blocks):
    keep = kn_hbm.shape[2]
    hd, b = pl.program_id(0), pl.program_id(1)
    new_copies = [
        pltpu.make_async_copy(src.at[pl.ds(seq - keep, keep), :], dst.at[0, b, :, hd, :], new_sem.at[n])
        for n, (src, dst) in enumerate(((k_ref, kn_hbm), (v_ref, vn_hbm)))]
    for cp in new_copies:
        cp.start()

    @pl.when(hd * pl.num_programs(1) + b < wd_blocks)
    def _():
        wdb_ref[...] = wd_ref[...].astype(BF16)

    @pl.when(b == 0)
    def _():
        t = _toeplitz_bias(r_ref[...], GROUP_ROWS)[:, :BAND_COLS]
        i = lax.broadcasted_iota(jnp.int32, (GROUP_ROWS, BAND_COLS), 0)
        j = lax.broadcasted_iota(jnp.int32, (GROUP_ROWS, BAND_COLS), 1)
        gap = j // CHUNK - i // CHUNK
        bias_ref[...] = jnp.where((gap >= 0) & (gap <= N_PREV_CHUNKS), t, NEG_INF)

    @pl.when((hd == 0) & (b == 0))
    def _():
        lane = lax.broadcasted_iota(jnp.int32, (seq, HEAD_DIM), 1)
        vb_ref[:, HEAD_DIM:] = jnp.where(lane == 0, 1.0, 0.0).astype(BF16)

    qb_ref[...] = q_ref[...].astype(BF16)
    kb_ref[...] = k_ref[...].astype(BF16)
    vb_ref[:, :HEAD_DIM] = v_ref[...].astype(BF16)
    for g in range(seq // GROUP_ROWS):
        q0 = g * GROUP_ROWS
        k0 = max(q0 - ATT_WINDOW, 0)
        nk = q0 + GROUP_ROWS - k0
        c0 = BAND_COLS - nk
        s = lax.dot_general(qb_ref[pl.ds(q0, GROUP_ROWS), :], kb_ref[pl.ds(k0, nk), :],
                            (((1,), (1,)), ((), ())), preferred_element_type=F32)
        s = s + bias_ref[:, c0:]
        m = jnp.max(s, axis=-1, keepdims=True)
        p = jnp.exp2(s - m).astype(BF16)
        o = jnp.dot(p, vb_ref[pl.ds(k0, nk), :], preferred_element_type=F32)
        o_ref[pl.ds(q0, GROUP_ROWS), :] = (o[:, :HEAD_DIM] / o[:, HEAD_DIM:HEAD_DIM + 1]).astype(BF16)
    for cp in new_copies:
        cp.wait()


def _prompt_attention(h, bias_rows, w_down, batch, seq, wd_rows):
    keep = min(ATT_WINDOW, seq)
    new_rows = lambda: pl.BlockSpec(memory_space=pl.ANY)
    new_shape = jax.ShapeDtypeStruct((1, batch, keep, N_HEADS, HEAD_DIM), F32)
    d_ff = w_down.shape[0]
    wd_blocks = d_ff // wd_rows
    assert wd_blocks * wd_rows == d_ff and wd_blocks <= N_HEADS * batch

    def col(seg):
        return pl.BlockSpec((seq, HEAD_DIM), lambda hd, b: (b, seg * N_HEADS + hd))

    wd_spec = lambda: pl.BlockSpec((wd_rows, D_MODEL),
                                   lambda hd, b: (jnp.minimum(hd * batch + b, wd_blocks - 1), 0))
    return pl.pallas_call(
        functools.partial(_prompt_attn_kernel, seq=seq, wd_blocks=wd_blocks),
        grid=(N_HEADS, batch),
        in_specs=[pl.BlockSpec((None, 1, BIAS_PERIOD), lambda hd, b: (hd, 0, 0)), col(0), col(1), col(2),
                  wd_spec()],
        out_specs=[pl.BlockSpec((seq, HEAD_DIM), lambda hd, b: (b, hd)), wd_spec(), new_rows(), new_rows()],
        out_shape=[jax.ShapeDtypeStruct((batch * seq, D_ATT), BF16),
                   jax.ShapeDtypeStruct((d_ff, D_MODEL), BF16), new_shape, new_shape],
        scratch_shapes=[pltpu.VMEM((GROUP_ROWS, BAND_COLS), F32)]
        + [pltpu.VMEM((seq, HEAD_DIM), BF16)] * 2 + [pltpu.VMEM((seq, 2 * HEAD_DIM), BF16)]
        + [pltpu.SemaphoreType.DMA((2,))],
        compiler_params=_params(("arbitrary", "arbitrary")),
        name="prompt_attention",
    )(bias_rows, h, h, h, w_down)


def _sample_attn_kernel(r_ref, q_ref, kn_ref, vn_ref, ck_ref, cv_ref, o_ref, bias_ref, kt_ref, vt_ref, *, lc, t):
    @pl.when(pl.program_id(0) == 0)
    def _():
        for h in range(N_HEADS):
            bias_ref[h] = _toeplitz_bias(r_ref[h], t)

    for src, dst in ((ck_ref, kt_ref), (cv_ref, vt_ref)):
        dst[...] = jnp.swapaxes(src[...].reshape(lc, N_HEADS, HEAD_DIM), 0, 1).astype(BF16)

    contract_last = (((1,), (1,)), ((), ()))
    for h in range(N_HEADS):
        cols = slice(h * HEAD_DIM, (h + 1) * HEAD_DIM)
        q = q_ref[:, cols].astype(BF16)
        kc = kt_ref[h]
        vc = vt_ref[h]
        kn = kn_ref[:, cols].astype(BF16)
        vn = vn_ref[:, cols].astype(BF16)
        s_c = lax.dot_general(q, kc, contract_last, preferred_element_type=F32) + bias_ref[h, :, :lc]
        s_n = lax.dot_general(q, kn, contract_last, preferred_element_type=F32) + bias_ref[h, :, lc:lc + t]
        m = jnp.maximum(jnp.max(s_c, axis=-1, keepdims=True), jnp.max(s_n, axis=-1, keepdims=True))
        p_c = jnp.exp2(s_c - m)
        p_n = jnp.exp2(s_n - m)
        l = jnp.sum(p_c, axis=-1, keepdims=True) + jnp.sum(p_n, axis=-1, keepdims=True)
        o = (jnp.dot(p_c.astype(BF16), vc, preferred_element_type=F32)
             + jnp.dot(p_n.astype(BF16), vn, preferred_element_type=F32))
        o_ref[:, cols] = (o / l).astype(BF16)


def _sample_attention(h, cache_k, cache_v, bias_rows, batch, t):
    lc = cache_k.shape[1] // N_HEADS
    assert lc == ATT_WINDOW and t <= CHUNK
    row = lambda seg: pl.BlockSpec((t, D_ATT), lambda b: (b, seg))
    cache = lambda: pl.BlockSpec((None, lc * N_HEADS, HEAD_DIM), lambda b: (b, 0, 0))
    return pl.pallas_call(
        functools.partial(_sample_attn_kernel, lc=lc, t=t),
        grid=(batch,),
        in_specs=[pl.BlockSpec((N_HEADS, 1, BIAS_PERIOD), lambda b: (0, 0, 0)),
                  row(0), row(1), row(2), cache(), cache()],
        out_specs=pl.BlockSpec((t, D_ATT), lambda b: (b, 0)),
        out_shape=jax.ShapeDtypeStruct((batch * t, D_ATT), BF16),
        scratch_shapes=[pltpu.VMEM((N_HEADS, t, BIAS_PERIOD), F32)]
        + [pltpu.VMEM((N_HEADS, lc, HEAD_DIM), BF16)] * 2,
        compiler_params=_params(("arbitrary",)),
        name="sample_attention",
    )(bias_rows, h, h, h, cache_k, cache_v)


HALO = POOL_HIST + 1
U_SEG = 3 * D_ATT // D_POOL
POOL_STAGES = max(POOL_WINDOWS).bit_length() - 1
POOL_HALO = 8 * POOL_STAGES


def _prompt_pool_kernel(u_ref, halo_ref, wp_ref, scale_ref, o_ref, full_ref, ping_ref, pong_ref, *, tm):
    t = pl.program_id(1)

    @pl.when(t == 0)
    def _():
        full_ref[0:POOL_HALO, :] = jnp.zeros((POOL_HALO, D_POOL), F32)

    @pl.when(t > 0)
    def _():
        full_ref[0:POOL_HALO, :] = halo_ref[...]

    full_ref[POOL_HALO:, :] = u_ref[...]
    rows = POOL_HALO + tm
    pos = t * tm + lax.broadcasted_iota(jnp.int32, (tm, 1), 0)
    for g, w in enumerate(POOL_WINDOWS):
        cols = slice(g * POOL_GROUP_W, (g + 1) * POOL_GROUP_W)
        src, src_cols = full_ref, cols
        for k in range(1, w.bit_length()):
            lo, back = 8 * k, 2 ** (k - 1)
            dst = (ping_ref, pong_ref)[k % 2]
            dst[lo:, :] = src[lo:, src_cols] + src[lo - back:rows - back, src_cols]
            src, src_cols = dst, slice(None)
        wsum = src[POOL_HALO:, src_cols]
        cnt = jnp.minimum(w, pos + 1).astype(F32)
        d = wsum / cnt - full_ref[POOL_HALO:, cols]
        y = jnp.dot(d.astype(BF16), wp_ref[g], preferred_element_type=F32)
        o_ref[:, cols] = (y * scale_ref[:, cols]).astype(BF16)


def _prompt_pool(h, w_pool_b, pool_scale, batch, seq, tm):
    tiles = seq // tm
    halo_per_tile = tm // POOL_HALO
    assert halo_per_tile * POOL_HALO == tm and POOL_HALO >= POOL_HIST
    stage = pltpu.VMEM((POOL_HALO + tm, POOL_GROUP_W), F32)
    return pl.pallas_call(
        functools.partial(_prompt_pool_kernel, tm=tm),
        grid=(batch, tiles),
        in_specs=[
            pl.BlockSpec((tm, D_POOL), lambda b, t: (b * tiles + t, U_SEG)),
            pl.BlockSpec((POOL_HALO, D_POOL),
                         lambda b, t: (jnp.maximum((b * tiles + t) * halo_per_tile - 1, 0), U_SEG)),
            pl.BlockSpec((len(POOL_WINDOWS), POOL_GROUP_W, POOL_GROUP_W), lambda b, t: (0, 0, 0)),
            pl.BlockSpec((1, D_POOL), lambda b, t: (0, 0)),
        ],
        out_specs=pl.BlockSpec((tm, D_POOL), lambda b, t: (b * tiles + t, 0)),
        out_shape=jax.ShapeDtypeStruct((batch * seq, D_POOL), BF16),
        scratch_shapes=[pltpu.VMEM((POOL_HALO + tm, D_POOL), F32), stage, stage],
        compiler_params=_params(("arbitrary", "arbitrary")),
        name="prompt_pool",
    )(h, h, w_pool_b, pool_scale)


def _sample_pool_kernel(u_ref, state_ref, wp_ref, scale_ref, o_ref, new_state_ref, full_ref, *, batch, t):
    full_ref[:, 1:HALO, :] = state_ref[...]
    full_ref[:, HALO:, :] = u_ref[...].reshape(batch, t, D_POOL)
    for g, w in enumerate(POOL_WINDOWS):
        cols = slice(g * POOL_GROUP_W, (g + 1) * POOL_GROUP_W)
        wsum = full_ref[:, HALO:, cols]
        for s in range(1, w):
            wsum = wsum + full_ref[:, HALO - s:HALO - s + t, cols]
        d = wsum / float(w) - full_ref[:, HALO:, cols]
        y = jnp.dot(d.reshape(batch * t, POOL_GROUP_W).astype(BF16), wp_ref[g],
                    preferred_element_type=F32)
        o_ref[:, cols] = (y * scale_ref[:, cols]).astype(BF16)
    new_state_ref[...] = full_ref[:, HALO + t - POOL_HIST:, :]


def _sample_pool(h, state, w_pool_b, pool_scale, batch, t):
    full = lambda shape: pl.BlockSpec(shape, lambda i: (0,) * len(shape))
    return pl.pallas_call(
        functools.partial(_sample_pool_kernel, batch=batch, t=t),
        grid=(1,),
        in_specs=[pl.BlockSpec((batch * t, D_POOL), lambda i: (0, U_SEG)),
                  full((batch, POOL_HIST, D_POOL)),
                  full((len(POOL_WINDOWS), POOL_GROUP_W, POOL_GROUP_W)), full((1, D_POOL))],
        out_specs=[full((batch * t, D_POOL)), full((batch, POOL_HIST, D_POOL))],
        out_shape=[jax.ShapeDtypeStruct((batch * t, D_POOL), BF16),
                   jax.ShapeDtypeStruct((batch, POOL_HIST, D_POOL), F32)],
        scratch_shapes=[pltpu.VMEM((batch, HALO + t, D_POOL), F32)],
        compiler_params=_params(("arbitrary",)),
        name="sample_pool",
    )(h, state, w_pool_b, pool_scale)


LN_ROWS = 128


def _layer_norm_rows(h, g, b):
    mu = jnp.mean(h, axis=-1, keepdims=True)
    c = h - mu
    var = jnp.mean(c * c, axis=-1, keepdims=True)
    return c * lax.rsqrt(var + LN_EPS) * g + b


N_STAGE = 2


def _chunk_rows(c, base=0):
    return pl.ds(pl.multiple_of(base + c * LN_ROWS, LN_ROWS), LN_ROWS)


def _layer_norm_to_hbm(src_ref, g_ref, b_ref, dst_hbm, row0, stage_ref, sem, after_chunk=None):
    n_chunks = src_ref.shape[0] // LN_ROWS
    assert n_chunks * LN_ROWS == src_ref.shape[0] and n_chunks % N_STAGE == 0

    def copy(c, slot):
        return pltpu.make_async_copy(stage_ref.at[slot], dst_hbm.at[_chunk_rows(c, row0), :], sem.at[slot])

    def body(c2, carry):
        for slot in range(N_STAGE):
            c = c2 * N_STAGE + slot

            @pl.when(c2 > 0)
            def _():
                copy(c - N_STAGE, slot).wait()

            stage_ref[slot] = _layer_norm_rows(src_ref[_chunk_rows(c), :], g_ref[...], b_ref[...])
            copy(c, slot).start()
            if after_chunk is not None:
                after_chunk(c)
        return carry

    lax.fori_loop(0, n_chunks // N_STAGE, body, 0)
    for slot in range(N_STAGE):
        copy(n_chunks - N_STAGE + slot, slot).wait()


def _stage_scratch():
    return [pltpu.VMEM((N_STAGE, LN_ROWS, D_MODEL), F32), pltpu.SemaphoreType.DMA((N_STAGE,))]


def _out_proj_kernel(att_ref, pool_ref, x_ref, w_ref, o_ref, *maybe_wb_ref):
    if maybe_wb_ref:
        wb_ref, = maybe_wb_ref
        wb_ref[...] = w_ref[...].astype(BF16)
    else:
        wb_ref = w_ref
    acc = jnp.dot(att_ref[...], wb_ref[:D_ATT, :], preferred_element_type=F32)
    acc = acc + jnp.dot(pool_ref[...], wb_ref[D_ATT:, :], preferred_element_type=F32)
    o_ref[...] = DEEPNORM_ALPHA * x_ref[...] + acc


def _out_proj(att_o, pool_o, x, w, tm, tn):
    m = x.shape[0]
    n_blocks = D_MODEL // tn
    cast_weights = w.dtype == F32
    blocked = pl.BlockSpec((None, D_MODEL, tn), lambda i, j: (j, 0, 0))
    out_specs = [pl.BlockSpec((tm, tn), lambda i, j: (i, j))]
    out_shape = [jax.ShapeDtypeStruct((m, D_MODEL), F32)]
    if cast_weights:
        assert m == tm
        w_spec = pl.BlockSpec((D_MODEL, tn), lambda i, j: (0, j))
        out_specs.append(blocked)
        out_shape.append(jax.ShapeDtypeStruct((n_blocks, D_MODEL, tn), BF16))
    else:
        w_spec = blocked
    return pl.pallas_call(
        _out_proj_kernel,
        grid=(m // tm, n_blocks),
        in_specs=[
            pl.BlockSpec((tm, D_ATT), lambda i, j: (i, 0)),
            pl.BlockSpec((tm, D_POOL), lambda i, j: (i, 0)),
            pl.BlockSpec((tm, tn), lambda i, j: (i, j)),
            w_spec,
        ],
        out_specs=out_specs,
        out_shape=out_shape,
        compiler_params=_params(("arbitrary", "arbitrary")),
        name="out_proj",
    )(att_o, pool_o, x, w)


DOWN_COLS = 512
SPLIT_LAG = 1


def _ffn_kernel(h_hbm, hx_hbm, wg_ref, wu_ref, wd_ref, g1_ref, b1_ref, g_ref, b_ref, y_hbm, yx_hbm,
                acc_ref, xb_ref, in_sem, stage_ref, out_sem, *, n_blocks, tm, tx, n_tiles):
    i = pl.program_id(0)
    j = pl.program_id(1)
    n_chunks = tm // LN_ROWS
    x_chunks = tx // LN_ROWS

    def x_copy(tile, c):
        return pltpu.make_async_copy(h_hbm.at[_chunk_rows(c, tile * tm), :], acc_ref.at[_chunk_rows(c), :],
                                     in_sem.at[c])

    def extra_copy(k):
        return pltpu.make_async_copy(hx_hbm.at[_chunk_rows(k), :], acc_ref.at[_chunk_rows(n_chunks + k), :],
                                     in_sem.at[n_chunks + k])

    def split(copy, c):
        copy.wait()
        x = _layer_norm_rows(acc_ref[_chunk_rows(c), :], g1_ref[...], b1_ref[...])
        xb_ref[_chunk_rows(c), :] = x.astype(BF16)
        acc_ref[_chunk_rows(c), :] = DEEPNORM_ALPHA * x

    @pl.when(j == 0)
    def _():
        @pl.when(i == 0)
        def _():
            for c in range(n_chunks):
                x_copy(0, c).start()
            for k in range(x_chunks):
                extra_copy(k).start()

        def split_rest(c, carry):
            split(x_copy(i, c), c)
            return carry

        lax.fori_loop(jnp.where(i == 0, 0, n_chunks - SPLIT_LAG), n_chunks, split_rest, 0)

        @pl.when(i == 0)
        def _():
            for k in range(x_chunks):
                split(extra_copy(k), n_chunks + k)

    def block_step(rows):
        xb = xb_ref[:rows, :]
        gate = jnp.dot(xb, wg_ref[...], preferred_element_type=F32)
        up = jnp.dot(xb, wu_ref[...], preferred_element_type=F32)
        act = (gate * (1.0 / (1.0 + jnp.exp(-gate))) * up).astype(BF16)
        for n in range(0, D_MODEL, DOWN_COLS):
            cols = slice(n, n + DOWN_COLS)
            acc_ref[:rows, cols] += jnp.dot(act, wd_ref[:, cols], preferred_element_type=F32)

    @pl.when(i == 0)
    def _():
        block_step(tm + tx)

    @pl.when(i > 0)
    def _():
        block_step(tm)

    @pl.when(j == n_blocks - 1)
    def _():
        def prefetch(c):
            @pl.when(i + 1 < n_tiles)
            def _():
                x_copy(i + 1, c).start()

                @pl.when(c >= SPLIT_LAG)
                def _():
                    split(x_copy(i + 1, c - SPLIT_LAG), c - SPLIT_LAG)

        _layer_norm_to_hbm(acc_ref.at[pl.ds(0, tm), :], g_ref, b_ref, y_hbm, i * tm, stage_ref, out_sem,
                           after_chunk=prefetch)

        @pl.when(i == 0)
        def _():
            _layer_norm_to_hbm(acc_ref.at[pl.ds(tm, tx), :], g_ref, b_ref, yx_hbm, 0, stage_ref, out_sem)


def _ffn_ln(h, hx, wg_blocks, wu_blocks, wd_b, g1, b1, g2, b2, tm):
    m, tx = h.shape[0], hx.shape[0]
    n_blocks, _, tf = wg_blocks.shape
    n_tiles = m // tm
    n_chunks, x_chunks = tm // LN_ROWS, tx // LN_ROWS
    assert n_tiles * tm == m and n_chunks * LN_ROWS == tm and x_chunks * LN_ROWS == tx
    assert SPLIT_LAG <= n_chunks
    return pl.pallas_call(
        functools.partial(_ffn_kernel, n_blocks=n_blocks, tm=tm, tx=tx, n_tiles=n_tiles),
        grid=(n_tiles, n_blocks),
        in_specs=[
            pl.BlockSpec(memory_space=pl.ANY),
            pl.BlockSpec(memory_space=pl.ANY),
            pl.BlockSpec((None, D_MODEL, tf), lambda i, j: (j, 0, 0)),
            pl.BlockSpec((None, D_MODEL, tf), lambda i, j: (j, 0, 0)),
            pl.BlockSpec((tf, D_MODEL), lambda i, j: (j, 0)),
        ] + [pl.BlockSpec((1, D_MODEL), lambda i, j: (0, 0))] * 4,
        out_specs=[pl.BlockSpec(memory_space=pl.ANY), pl.BlockSpec(memory_space=pl.ANY)],
        out_shape=[jax.ShapeDtypeStruct((m, D_MODEL), F32), jax.ShapeDtypeStruct((tx, D_MODEL), F32)],
        scratch_shapes=[pltpu.VMEM((tm + tx, D_MODEL), F32), pltpu.VMEM((tm + tx, D_MODEL), BF16),
                        pltpu.SemaphoreType.DMA((n_chunks + x_chunks,))] + _stage_scratch(),
        compiler_params=_params(("arbitrary", "arbitrary")),
        name="ffn_ln",
    )(h, hx, wg_blocks, wu_blocks, wd_b, g1, b1, g2, b2)


IN_PROJ_COLS = 512
OUT_PROJ_COLS = 1024
FFN_COLS = 256
ROW_TILE = 1024
W_DOWN_CAST_ROWS = 256
POOL_ROWS = 1024


def kernel(x_prompt, x_sample, cache_k, cache_v, state_pool, w_in, rel_bias, w_pool, pool_scale,
           w_out, ln1_g, ln1_b, w_gate, w_up, w_down, ln2_g, ln2_b):
    batch, seq, _ = x_prompt.shape
    dec_batch, dec_seq, _ = x_sample.shape
    depth = w_in.shape[0]
    assert depth == 1 and seq % GROUP_ROWS == 0 and seq >= ATT_WINDOW
    lc = cache_k.shape[2]
    n_dec = dec_batch * dec_seq

    n_groups = len(POOL_WINDOWS)
    w_pool_b = _cast_rows(w_pool[0].reshape(n_groups * POOL_GROUP_W, POOL_GROUP_W),
                          n_groups * POOL_GROUP_W).reshape(n_groups, POOL_GROUP_W, POOL_GROUP_W)
    bias_rows = _bias_rows(rel_bias[0])
    g1, b1 = ln1_g[0][None], ln1_b[0][None]
    g2, b2 = ln2_g[0][None], ln2_b[0][None]
    scale = pool_scale[0][None]
    xp = x_prompt.reshape(batch * seq, D_MODEL)
    xs = x_sample.reshape(n_dec, D_MODEL)

    hs, w_in_b = _in_proj_cast(xs, w_in[0], IN_PROJ_COLS)
    hp, w_gate_b, w_up_b = _in_proj(xp, w_in_b, w_gate[0], w_up[0], ROW_TILE, FFN_COLS)

    att_s = _sample_attention(hs, cache_k[0].reshape(dec_batch, lc * N_HEADS, HEAD_DIM),
                              cache_v[0].reshape(dec_batch, lc * N_HEADS, HEAD_DIM),
                              bias_rows, dec_batch, dec_seq)
    pool_s, ps_new = _sample_pool(hs, state_pool[0], w_pool_b, scale, dec_batch, dec_seq)
    res_s, w_out_b = _out_proj(att_s, pool_s, xs, w_out[0], n_dec, OUT_PROJ_COLS)

    att_p, w_down_b, kp_new, vp_new = _prompt_attention(hp, bias_rows, w_down[0], batch, seq,
                                                        W_DOWN_CAST_ROWS)
    pool_p = _prompt_pool(hp, w_pool_b, scale, batch, seq, POOL_ROWS)
    res_p, = _out_proj(att_p, pool_p, xp, w_out_b, ROW_TILE, OUT_PROJ_COLS)

    yp, ys = _ffn_ln(res_p, res_s, w_gate_b, w_up_b, w_down_b, g1, b1, g2, b2, ROW_TILE)

    hp3 = hp.reshape(batch, seq, 4 * D_ATT)
    dec_shape = (1, dec_batch, dec_seq, N_HEADS, HEAD_DIM)
    return (yp.reshape(batch, seq, D_MODEL),
            ys.reshape(dec_batch, dec_seq, D_MODEL),
            kp_new,
            vp_new,
            hp3[None, :, seq - POOL_HIST:, 3 * D_ATT:],
            hs[:, D_ATT:2 * D_ATT].reshape(dec_shape),
            hs[:, 2 * D_ATT:3 * D_ATT].reshape(dec_shape),
            ps_new[None])
```
